```python
import jax, jax.numpy as jnp
from jax import lax
import numpy as np

D_MODEL = 1024
BATCH = 8
SEQ = 2048
DEPTH = 4
DEC_BATCH = 128
DEC_SEQ = 1
PAST_LEN = 16384
PAGE_SIZE = 128

W_A = D_MODEL
W_B = D_MODEL
W_C = 2 * D_MODEL
POOL_WINDOWS = (2, 4, 8, 16)
POOL_GROUPS = 4
POOL_GW = W_A // POOL_GROUPS
POOL_CTX = 15
SGU_GROUPS = 4
SGU_GW = W_B // SGU_GROUPS
CHUNK = 128
CONV_WIDTH = 3
CONV_CTX = CONV_WIDTH - 1
N_MEM = 256
XA_HEADS = 4
XA_HD = D_MODEL // XA_HEADS
EPS = 1e-6

kernel_name = "hybrid_pool_sgu_conv_memxattn_step"


def rms_norm(x, g):
    xf = x.astype(jnp.float32)
    y = xf * lax.rsqrt(jnp.mean(xf * xf, axis=-1, keepdims=True) + EPS)
    return (y * g.astype(jnp.float32)).astype(x.dtype)


def causal_multi_pool(ext, n_new, pos0):
    n_ctx = ext.shape[1] - n_new
    cs = jnp.cumsum(ext.astype(jnp.float32), axis=1)
    cs = jnp.pad(cs, ((0, 0), (1, 0), (0, 0)))
    pos = pos0 + jnp.arange(n_new)
    outs = []
    for g, w in enumerate(POOL_WINDOWS):
        sl = slice(g * POOL_GW, (g + 1) * POOL_GW)
        hi = cs[:, n_ctx + 1:n_ctx + 1 + n_new, sl]
        lo = cs[:, n_ctx + 1 - w:n_ctx + 1 - w + n_new, sl]
        cnt = jnp.minimum(pos + 1, w).astype(jnp.float32)[None, :, None]
        outs.append((hi - lo) / cnt)
    mean = jnp.concatenate(outs, axis=-1)
    return (mean - ext[:, n_ctx:].astype(jnp.float32)).astype(ext.dtype)


def chunk_spatial_mix(v, w_s, b_s):
    b, L, _ = v.shape
    n_chunks = -(-L // CHUNK)
    lp = n_chunks * CHUNK
    vp = jnp.pad(v, ((0, 0), (0, lp - L), (0, 0))).reshape(b, n_chunks, CHUNK, SGU_GROUPS, SGU_GW)
    mask = jnp.tril(jnp.ones((CHUNK, CHUNK), dtype=bool))
    w = jnp.where(mask[None], w_s, jnp.zeros_like(w_s))
    mixed = jnp.einsum('gts,bnsgc->bntgc', w, vp) + b_s.T[None, None, :, :, None]
    return mixed.reshape(b, lp, W_B)[:, :L]


def pool_sgu_mixer(h, pool_buf, pos0, w_in, pool_maps, pool_scale, sgu_w, sgu_b, sgu_g, w_out):
    b, L, _ = h.shape
    z = h @ w_in
    xa = z[..., :W_A]
    ga = z[..., W_A:2 * W_A]
    u = z[..., 2 * W_A:2 * W_A + W_B]
    v = z[..., 2 * W_A + W_B:2 * W_A + 2 * W_B]
    gb = z[..., 2 * W_A + 2 * W_B:]
    ext = jnp.concatenate([pool_buf, xa], axis=1)
    pooled = causal_multi_pool(ext, L, pos0).reshape(b, L, POOL_GROUPS, POOL_GW)
    ya = jnp.einsum('blgc,gcd->blgd', pooled, pool_maps).reshape(b, L, W_A) * pool_scale
    ya = ya * jax.nn.silu(ga)
    vn = rms_norm(v, sgu_g)
    yb = u * chunk_spatial_mix(vn, sgu_w, sgu_b) * jax.nn.silu(gb)
    out = jnp.concatenate([ya, yb], axis=-1) @ w_out
    return out, ext[:, -POOL_CTX:], vn


def short_conv_mixer(h, conv_buf, w_in, conv_w, w_out):
    L = h.shape[1]
    z = h @ w_in
    bg = z[..., :W_C]
    cg = z[..., W_C:2 * W_C]
    xc = z[..., 2 * W_C:3 * W_C]
    g = z[..., 3 * W_C:]
    ext = jnp.concatenate([conv_buf, cg * xc], axis=1)
    y = conv_w[0] * ext[:, 0:L] + conv_w[1] * ext[:, 1:L + 1] + conv_w[2] * ext[:, 2:L + 2]
    out = (bg * y * jax.nn.silu(g)) @ w_out
    return out, ext[:, -CONV_CTX:]


def memory_kv(mem, g, w_k, w_v):
    b, m, _ = mem.shape
    mn = rms_norm(mem, g)
    k = (mn @ w_k).reshape(b, m, XA_HEADS, XA_HD)
    v = (mn @ w_v).reshape(b, m, XA_HEADS, XA_HD)
    return k, v


def cross_attend(h, k, v, w_q, w_o):
    b, L, _ = h.shape
    q = (h @ w_q).reshape(b, L, XA_HEADS, XA_HD)
    s = jnp.einsum('blhd,bmhd->bhlm', q, k).astype(jnp.float32) * (XA_HD ** -0.5)
    p = jax.nn.softmax(s, axis=-1).astype(v.dtype)
    o = jnp.einsum('bhlm,bmhd->blhd', p, v).reshape(b, L, D_MODEL)
    return o @ w_o


def setup_inputs(seed: int = 0) -> dict:
    key = jax.random.key(seed)
    ks = jax.random.split(key, 26)
    n_even = (DEPTH + 1) // 2
    n_odd = DEPTH // 2

    def nrm(k, shape, scale=1.0):
        return jax.random.normal(k, shape, jnp.float32) * scale

    return {
        "x_prompt": nrm(ks[0], (BATCH, SEQ, D_MODEL)),
        "x_sample": nrm(ks[1], (DEC_BATCH, DEC_SEQ, D_MODEL)),
        "mem_prompt": nrm(ks[2], (BATCH, N_MEM, D_MODEL)),
        "state_pool": nrm(ks[3], (n_even, DEC_BATCH, POOL_CTX, W_A)),
        "state_conv": nrm(ks[4], (n_odd, DEC_BATCH, CONV_CTX, W_C)),
        "cache_mem_k": nrm(ks[5], (DEPTH, DEC_BATCH, N_MEM, XA_HEADS, XA_HD)),
        "cache_mem_v": nrm(ks[6], (DEPTH, DEC_BATCH, N_MEM, XA_HEADS, XA_HD)),
        "norm_mix_g": 1.0 + nrm(ks[7], (DEPTH, D_MODEL), 0.05),
        "norm_xattn_g": 1.0 + nrm(ks[8], (DEPTH, D_MODEL), 0.05),
        "norm_mem_g": 1.0 + nrm(ks[9], (DEPTH, D_MODEL), 0.05),
        "w_in_ab": nrm(ks[10], (n_even, D_MODEL, 2 * W_A + 3 * W_B), D_MODEL ** -0.5),
        "pool_maps": nrm(ks[11], (n_even, POOL_GROUPS, POOL_GW, POOL_GW), POOL_GW ** -0.5),
        "pool_scale": 0.5 + nrm(ks[12], (n_even, W_A), 0.05),
        "sgu_w": nrm(ks[13], (n_even, SGU_GROUPS, CHUNK, CHUNK), CHUNK ** -0.5),
        "sgu_b": 1.0 + nrm(ks[14], (n_even, SGU_GROUPS, CHUNK), 0.1),
        "sgu_g": 1.0 + nrm(ks[15], (n_even, W_B), 0.05),
        "w_out_ab": nrm(ks[16], (n_even, W_A + W_B, D_MODEL), (W_A + W_B) ** -0.5),
        "w_in_c": nrm(ks[17], (n_odd, D_MODEL, 4 * W_C), D_MODEL ** -0.5),
        "conv_w": nrm(ks[18], (n_odd, CONV_WIDTH, W_C), CONV_WIDTH ** -0.5),
        "w_out_c": nrm(ks[19], (n_odd, W_C, D_MODEL), W_C ** -0.5),
        "w_q": nrm(ks[20], (DEPTH, D_MODEL, D_MODEL), D_MODEL ** -0.5),
        "w_k": nrm(ks[21], (DEPTH, D_MODEL, D_MODEL), D_MODEL ** -0.5),
        "w_v": nrm(ks[22], (DEPTH, D_MODEL, D_MODEL), D_MODEL ** -0.5),
        "w_o": nrm(ks[23], (DEPTH, D_MODEL, D_MODEL), D_MODEL ** -0.5),
        "norm_final_g": 1.0 + nrm(ks[24], (D_MODEL,), 0.05),
    }


def reference(x_prompt, x_sample, mem_prompt, state_pool, state_conv, cache_mem_k, cache_mem_v,
              norm_mix_g, norm_xattn_g, norm_mem_g, w_in_ab, pool_maps, pool_scale, sgu_w, sgu_b,
              sgu_g, w_out_ab, w_in_c, conv_w, w_out_c, w_q, w_k, w_v, w_o, norm_final_g):
    bp = x_prompt.shape[0]
    xp, xs = x_prompt, x_sample
    pool_p, pool_s, conv_p, conv_s, vrows_s, mem_k_p, mem_v_p = [], [], [], [], [], [], []
    for i in range(DEPTH):
        j = i // 2
        hp = rms_norm(xp, norm_mix_g[i])
        hs = rms_norm(xs, norm_mix_g[i])
        if i % 2 == 0:
            prm = (w_in_ab[j], pool_maps[j], pool_scale[j], sgu_w[j], sgu_b[j], sgu_g[j], w_out_ab[j])
            zero_buf = jnp.zeros((bp, POOL_CTX, W_A), xp.dtype)
            op, buf_p, _ = pool_sgu_mixer(hp, zero_buf, 0, *prm)
            os_, buf_s, v_s = pool_sgu_mixer(hs, state_pool[j], PAST_LEN, *prm)
            pool_p.append(buf_p)
            pool_s.append(buf_s)
            vrows_s.append(v_s)
        else:
            zero_buf = jnp.zeros((bp, CONV_CTX, W_C), xp.dtype)
            op, buf_p = short_conv_mixer(hp, zero_buf, w_in_c[j], conv_w[j], w_out_c[j])
            os_, buf_s = short_conv_mixer(hs, state_conv[j], w_in_c[j], conv_w[j], w_out_c[j])
            conv_p.append(buf_p)
            conv_s.append(buf_s)
        xp = xp + op
        xs = xs + os_
        kp, vp = memory_kv(mem_prompt, norm_mem_g[i], w_k[i], w_v[i])
        mem_k_p.append(kp)
        mem_v_p.append(vp)
        xp = xp + cross_attend(rms_norm(xp, norm_xattn_g[i]), kp, vp, w_q[i], w_o[i])
        xs = xs + cross_attend(rms_norm(xs, norm_xattn_g[i]), cache_mem_k[i], cache_mem_v[i], w_q[i], w_o[i])
    y_prompt = rms_norm(xp, norm_final_g)
    y_sample = rms_norm(xs, norm_final_g)
    return (y_prompt, y_sample, jnp.stack(pool_p), jnp.stack(pool_s), jnp.stack(conv_p), jnp.stack(conv_s),
            jnp.stack(vrows_s), jnp.stack(mem_k_p), jnp.stack(mem_v_p))
```

```python
import functools

import jax
import jax.numpy as jnp
from jax import lax
from jax.experimental import pallas as pl
from jax.experimental.pallas import tpu as pltpu

POOL_WINDOWS = (2, 4, 8, 16)
POOL_GROUPS = len(POOL_WINDOWS)
POOL_CTX = max(POOL_WINDOWS) - 1
SGU_GROUPS = 4
CHUNK = 128
CONV_WIDTH = 3
CONV_CTX = CONV_WIDTH - 1
XA_HEADS = 4
EPS = 1e-6

V7X_VMEM_LIMIT_BYTES = 60 * 1024 * 1024
SUBLANES = 8
LANES = 128
PAST_LEN = 16384
POOL_HIST = 16
CONV_HIST = 8

PROMPT_ROWS = 256
SAMPLE_BLOCK = 8

F32 = jnp.float32
BF16 = jnp.bfloat16


def _rms(x, g):
    return x * lax.rsqrt(jnp.mean(x * x, axis=-1, keepdims=True) + EPS) * g


def _silu(x):
    return x * jax.nn.sigmoid(x)


def _dot(a, b):
    return jnp.dot(a, b, preferred_element_type=F32)


def _resident(shape):
    zeros = (0,) * len(shape)
    return pl.BlockSpec(shape, lambda *_: zeros, pipeline_mode=pl.Buffered(1))


def _params(n_grid):
    return pltpu.CompilerParams(dimension_semantics=("arbitrary",) * n_grid,
                                vmem_limit_bytes=V7X_VMEM_LIMIT_BYTES)


def _memkv_kernel(mem_ref, g_ref, wk_ref, wv_ref, k_ref, v_ref, kt_ref, vb_ref):
    mn = _rms(mem_ref[0], g_ref[0]).astype(BF16)
    k = _dot(mn, wk_ref[0])
    v = _dot(mn, wv_ref[0])
    k_ref[0, 0] = k
    v_ref[0, 0] = v
    kt_ref[0, 0] = k.T.astype(BF16)
    vb_ref[0, 0] = v.astype(BF16)


def _memory_kv(mem, norm_g, w_k, w_v):
    depth, d = norm_g.shape
    b, m, _ = mem.shape
    wspec = pl.BlockSpec((1, d, d), lambda l, i: (l, 0, 0))
    return pl.pallas_call(
        _memkv_kernel,
        grid=(depth, b),
        in_specs=[pl.BlockSpec((1, m, d), lambda l, i: (i, 0, 0)),
                  pl.BlockSpec((1, 1, d), lambda l, i: (l, 0, 0)),
                  wspec, wspec],
        out_specs=[pl.BlockSpec((1, 1, m, d), lambda l, i: (l, i, 0, 0)),
                   pl.BlockSpec((1, 1, m, d), lambda l, i: (l, i, 0, 0)),
                   pl.BlockSpec((1, 1, d, m), lambda l, i: (l, i, 0, 0)),
                   pl.BlockSpec((1, 1, m, d), lambda l, i: (l, i, 0, 0))],
        out_shape=[jax.ShapeDtypeStruct((depth, b, m, d), F32),
                   jax.ShapeDtypeStruct((depth, b, m, d), F32),
                   jax.ShapeDtypeStruct((depth, b, d, m), BF16),
                   jax.ShapeDtypeStruct((depth, b, m, d), BF16)],
        compiler_params=_params(2),
        name="memory_kv",
    )(mem, norm_g.reshape(depth, 1, d), w_k, w_v)


def _even_prompt_kernel(x_ref, g_ref, w_in_ref, maps_ref, pscale_ref, sgw_ref, sgbias_ref, sgg_ref,
                        w_out_ref, xo_ref, tail_ref, z_ref, y_ref, hist_ref):
    rows, w_a = y_ref.shape[0], pscale_ref.shape[1]
    w_b = sgg_ref.shape[1]
    gw = w_a // POOL_GROUPS
    sw = w_b // SGU_GROUPS
    s = pl.program_id(1)

    @pl.when(s == 0)
    def _():
        hist_ref[...] = jnp.zeros_like(hist_ref)

    x = x_ref[0]
    h = _rms(x, g_ref[...]).astype(BF16)
    z_ref[...] = _dot(h, w_in_ref[...])

    pos = s * rows + lax.broadcasted_iota(jnp.int32, (rows, gw), 0)
    for g, w in enumerate(POOL_WINDOWS):
        cols = slice(g * gw, (g + 1) * gw)
        xa = z_ref[:, cols]
        acc = jnp.concatenate([hist_ref[:, cols], xa], axis=0)
        shift = 1
        while shift < w:
            acc = acc + pltpu.roll(acc, shift, 0)
            shift *= 2
        cnt = jnp.minimum(pos + 1, w).astype(F32)
        pooled = acc[POOL_HIST:] / cnt - xa
        ya = _dot(pooled.astype(BF16), maps_ref[g]) * pscale_ref[:, cols]
        ya = ya * _silu(z_ref[:, w_a + g * gw:w_a + (g + 1) * gw])
        y_ref[:, cols] = ya.astype(BF16)
    tail = z_ref[rows - POOL_HIST:, 0:w_a]
    hist_ref[...] = tail
    tail_ref[0] = tail

    vn = _rms(z_ref[:, 2 * w_a + w_b:2 * w_a + 2 * w_b], sgg_ref[...]).astype(BF16)
    causal = (lax.broadcasted_iota(jnp.int32, (CHUNK, CHUNK), 0)
              >= lax.broadcasted_iota(jnp.int32, (CHUNK, CHUNK), 1))
    for g in range(SGU_GROUPS):
        cols = slice(g * sw, (g + 1) * sw)
        wg = jnp.where(causal, sgw_ref[g], 0.0).astype(BF16)
        for c in range(rows // CHUNK):
            rs = slice(c * CHUNK, (c + 1) * CHUNK)
            mixed = _dot(wg, vn[rs, cols]) + sgbias_ref[:, cols]
            u = z_ref[rs, 2 * w_a + g * sw:2 * w_a + (g + 1) * sw]
            gb = z_ref[rs, 2 * w_a + 2 * w_b + g * sw:2 * w_a + 2 * w_b + (g + 1) * sw]
            y_ref[rs, w_a + g * sw:w_a + (g + 1) * sw] = (u * mixed * _silu(gb)).astype(BF16)

    xo_ref[0] = x + _dot(y_ref[...], w_out_ref[...])


def _even_prompt(x, norm_g, w_in, maps, pscale, sgw, sgbias, sgg, w_out):
    b, seq, d = x.shape
    w_a, w_b = pscale.shape[-1], sgg.shape[-1]
    rows = PROMPT_ROWS
    xspec = pl.BlockSpec((1, rows, d), lambda i, s: (i, s, 0))
    return pl.pallas_call(
        _even_prompt_kernel,
        grid=(b, seq // rows),
        in_specs=[xspec, _resident((1, d)), _resident(w_in.shape), _resident(maps.shape),
                  _resident((1, w_a)), _resident(sgw.shape), _resident(sgbias.shape),
                  _resident((1, w_b)), _resident(w_out.shape)],
        out_specs=[xspec, pl.BlockSpec((1, POOL_HIST, w_a), lambda i, s: (i, 0, 0))],
        out_shape=[jax.ShapeDtypeStruct(x.shape, F32),
                   jax.ShapeDtypeStruct((b, POOL_HIST, w_a), F32)],
        scratch_shapes=[pltpu.VMEM((rows, w_in.shape[1]), F32),
                        pltpu.VMEM((rows, w_a + w_b), BF16),
                        pltpu.VMEM((POOL_HIST, w_a), F32)],
        compiler_params=_params(2),
        name="even_prompt",
    )(x, norm_g.reshape(1, d), w_in, maps, pscale.reshape(1, w_a), sgw, sgbias,
      sgg.reshape(1, w_b), w_out)


def _odd_prompt_kernel(x_ref, g_ref, w_in_ref, cw_ref, w_out_ref, xo_ref, tail_ref,
                       z_ref, y_ref, hist_ref, *, col_slab):
    rows, w_c = y_ref.shape
    s = pl.program_id(1)

    @pl.when(s == 0)
    def _():
        hist_ref[...] = jnp.zeros_like(hist_ref)

    x = x_ref[0]
    h = _rms(x, g_ref[...]).astype(BF16)
    z_ref[...] = _dot(h, w_in_ref[...])
    for c0 in range(0, w_c, col_slab):
        cols = slice(c0, c0 + col_slab)
        e = z_ref[:, w_c + c0:w_c + c0 + col_slab] * z_ref[:, 2 * w_c + c0:2 * w_c + c0 + col_slab]
        ext = jnp.concatenate([hist_ref[:, cols], e], axis=0)
        y = (cw_ref[0:1, cols] * pltpu.roll(ext, 2, 0)[CONV_HIST:]
             + cw_ref[1:2, cols] * pltpu.roll(ext, 1, 0)[CONV_HIST:]
             + cw_ref[2:3, cols] * e)
        tail = e[rows - CONV_HIST:]
        hist_ref[:, cols] = tail
        tail_ref[0, :, cols] = tail
        gate = _silu(z_ref[:, 3 * w_c + c0:3 * w_c + c0 + col_slab])
        y_ref[:, cols] = (z_ref[:, cols] * y * gate).astype(BF16)
    xo_ref[0] = x + _dot(y_ref[...], w_out_ref[...])


def _odd_prompt(x, norm_g, w_in, conv_w, w_out):
    b, seq, d = x.shape
    w_c = conv_w.shape[-1]
    rows = PROMPT_ROWS
    xspec = pl.BlockSpec((1, rows, d), lambda i, s: (i, s, 0))
    return pl.pallas_call(
        functools.partial(_odd_prompt_kernel, col_slab=512),
        grid=(b, seq // rows),
        in_specs=[xspec, _resident((1, d)), _resident(w_in.shape), _resident(conv_w.shape),
                  _resident(w_out.shape)],
        out_specs=[xspec, pl.BlockSpec((1, CONV_HIST, w_c), lambda i, s: (i, 0, 0))],
        out_shape=[jax.ShapeDtypeStruct(x.shape, F32),
                   jax.ShapeDtypeStruct((b, CONV_HIST, w_c), F32)],
        scratch_shapes=[pltpu.VMEM((rows, w_in.shape[1]), F32),
                        pltpu.VMEM((rows, w_c), BF16),
                        pltpu.VMEM((CONV_HIST, w_c), F32)],
        compiler_params=_params(2),
        name="odd_prompt",
    )(x, norm_g.reshape(1, d), w_in, conv_w, w_out)


def _attn_prompt_kernel(x_ref, g_ref, wq_ref, kt_ref, v_ref, wo_ref, gf_ref, xo_ref, o_ref, *, final):
    d = x_ref.shape[-1]
    hd = d // XA_HEADS
    x = x_ref[0]
    h = _rms(x, g_ref[...]).astype(BF16)
    q = _dot(h, wq_ref[...]).astype(BF16)
    for hh in range(XA_HEADS):
        cols = slice(hh * hd, (hh + 1) * hd)
        s = _dot(q[:, cols], kt_ref[0, cols, :]) * (hd ** -0.5)
        e = jnp.exp(s - jnp.max(s, axis=-1, keepdims=True))
        p = e / jnp.sum(e, axis=-1, keepdims=True)
        o_ref[:, cols] = _dot(p.astype(BF16), v_ref[0, :, cols]).astype(BF16)
    xn = x + _dot(o_ref[...], wo_ref[...])
    if final:
        xn = _rms(xn, gf_ref[...])
    xo_ref[0] = xn


def _attn_prompt(x, norm_g, w_q, kt, vb, w_o, final_g, final):
    b, seq, d = x.shape
    m = vb.shape[1]
    rows = PROMPT_ROWS
    xspec = pl.BlockSpec((1, rows, d), lambda i, s: (i, s, 0))
    return pl.pallas_call(
        functools.partial(_attn_prompt_kernel, final=final),
        grid=(b, seq // rows),
        in_specs=[xspec, _resident((1, d)), _resident(w_q.shape),
                  pl.BlockSpec((1, d, m), lambda i, s: (i, 0, 0)),
                  pl.BlockSpec((1, m, d), lambda i, s: (i, 0, 0)),
                  _resident(w_o.shape), _resident((1, d))],
        out_specs=xspec,
        out_shape=jax.ShapeDtypeStruct(x.shape, F32),
        scratch_shapes=[pltpu.VMEM((rows, d), BF16)],
        compiler_params=_params(2),
        name="attn_prompt",
    )(x, norm_g.reshape(1, d), w_q, kt, vb, w_o, final_g.reshape(1, d))


def _even_sample_kernel(xs_ref, st_ref, g_ref, w_in_ref, maps_ref, pscale_ref, w00_ref, b0_ref,
                        sgg_ref, w_out_ref, xo_ref, pool_ref, vn_ref, y_ref, *, pos0):
    w_a, w_b = pscale_ref.shape[1], sgg_ref.shape[1]
    gw = w_a // POOL_GROUPS
    xs = xs_ref[...]
    h = _rms(xs, g_ref[...]).astype(BF16)
    xa = _dot(h, w_in_ref[:, 0:w_a])
    ga = _dot(h, w_in_ref[:, w_a:2 * w_a])
    u = _dot(h, w_in_ref[:, 2 * w_a:2 * w_a + w_b])
    v = _dot(h, w_in_ref[:, 2 * w_a + w_b:2 * w_a + 2 * w_b])
    gb = _dot(h, w_in_ref[:, 2 * w_a + 2 * w_b:])

    for g, w in enumerate(POOL_WINDOWS):
        cols = slice(g * gw, (g + 1) * gw)
        acc = xa[:, cols]
        for k in range(POOL_CTX - (w - 1), POOL_CTX):
            acc = acc + st_ref[k, :, cols]
        pooled = acc / float(min(pos0 + 1, w)) - xa[:, cols]
        ya = _dot(pooled.astype(BF16), maps_ref[g]) * pscale_ref[:, cols] * _silu(ga[:, cols])
        y_ref[:, cols] = ya.astype(BF16)
    pool_ref[0:POOL_CTX - 1] = st_ref[1:POOL_CTX]
    pool_ref[POOL_CTX - 1] = xa

    vn = _rms(v, sgg_ref[...])
    vn_ref[...] = vn
    mixed = w00_ref[...] * vn + b0_ref[...]
    y_ref[:, w_a:] = (u * mixed * _silu(gb)).astype(BF16)
    xo_ref[...] = xs + _dot(y_ref[...], w_out_ref[...])


def _even_sample(xs, state, norm_g, w_in, maps, pscale, w00, b0, sgg, w_out, pos0):
    n, d = xs.shape
    w_a, w_b = pscale.shape[-1], sgg.shape[-1]
    return pl.pallas_call(
        functools.partial(_even_sample_kernel, pos0=pos0),
        out_shape=[jax.ShapeDtypeStruct((n, d), F32),
                   jax.ShapeDtypeStruct((POOL_CTX, n, w_a), F32),
                   jax.ShapeDtypeStruct((n, w_b), F32)],
        scratch_shapes=[pltpu.VMEM((n, w_a + w_b), BF16)],
        compiler_params=_params(0),
        name="even_sample",
    )(xs, state, norm_g.reshape(1, d), w_in, maps, pscale.reshape(1, w_a), w00, b0,
      sgg.reshape(1, w_b), w_out)


def _odd_sample_kernel(xs_ref, st_ref, g_ref, w_in_ref, cw_ref, w_out_ref, xo_ref, conv_ref, y_ref):
    w_c = cw_ref.shape[1]
    xs = xs_ref[...]
    h = _rms(xs, g_ref[...]).astype(BF16)
    bg = _dot(h, w_in_ref[:, 0:w_c])
    cg = _dot(h, w_in_ref[:, w_c:2 * w_c])
    xc = _dot(h, w_in_ref[:, 2 * w_c:3 * w_c])
    gate = _dot(h, w_in_ref[:, 3 * w_c:])
    e = cg * xc
    y = cw_ref[0:1, :] * st_ref[:, 0:w_c] + cw_ref[1:2, :] * st_ref[:, w_c:] + cw_ref[2:3, :] * e
    conv_ref[:, 0:w_c] = st_ref[:, w_c:]
    conv_ref[:, w_c:] = e
    y_ref[...] = (bg * y * _silu(gate)).astype(BF16)
    xo_ref[...] = xs + _dot(y_ref[...], w_out_ref[...])


def _odd_sample(xs, state, norm_g, w_in, conv_w, w_out):
    n, d = xs.shape
    w_c = conv_w.shape[-1]
    return pl.pallas_call(
        _odd_sample_kernel,
        out_shape=[jax.ShapeDtypeStruct((n, d), F32),
                   jax.ShapeDtypeStruct((n, CONV_CTX * w_c), F32)],
        scratch_shapes=[pltpu.VMEM((n, w_c), BF16)],
        compiler_params=_params(0),
        name="odd_sample",
    )(xs, state, norm_g.reshape(1, d), w_in, conv_w, w_out)


def _lane_tiled(cache):
    l, b, m, h, hd = cache.shape
    t = hd // LANES
    return cache.reshape(l, b, m, h, t, LANES).transpose(0, 1, 2, 4, 3, 5).reshape(l, b, m, t * h, LANES)


def _attn_sample_kernel(xs_ref, g_ref, wq_ref, k_ref, v_ref, wo_ref, gf_ref, xo_ref, q_ref, o_ref, *, final):
    nb, d = xs_ref.shape
    m, rows = k_ref.shape[2], k_ref.shape[3]
    hd = d // XA_HEADS
    n = m * rows
    feat = lambda r: (r % XA_HEADS) * hd + (r // XA_HEADS) * LANES

    xs = xs_ref[...]
    h = _rms(xs, g_ref[...]).astype(BF16)
    q = _dot(h, wq_ref[...])
    for r in range(rows):
        q_ref[r * nb:(r + 1) * nb, :] = q[:, feat(r):feat(r) + LANES]

    lane = lax.broadcasted_iota(jnp.int32, (rows, n), 1)
    diag = lane % rows == lax.broadcasted_iota(jnp.int32, (rows, n), 0)
    upper = lane[0:1] % rows >= XA_HEADS
    for i in range(nb):
        qt = q_ref[pl.ds(i, rows, stride=nb), :].astype(BF16)
        k2 = k_ref[0, i].reshape(n, LANES).astype(BF16)
        s_all = lax.dot_general(qt, k2, (((1,), (1,)), ((), ())), preferred_element_type=F32)
        t = jnp.sum(jnp.where(diag, s_all, 0.0), axis=0, keepdims=True)
        u = t + pltpu.roll(t, XA_HEADS, 1)
        s = jnp.where(upper, u, pltpu.roll(u, n - XA_HEADS, 1)) * (hd ** -0.5)
        parts = [s[:, j * LANES:(j + 1) * LANES] for j in range(n // LANES)]
        mx = functools.reduce(jnp.maximum, parts)
        shift = rows
        while shift < LANES:
            mx = jnp.maximum(mx, pltpu.roll(mx, shift, 1))
            shift *= 2
        es = [jnp.exp(part - mx) for part in parts]
        den = functools.reduce(jnp.add, es)
        shift = rows
        while shift < LANES:
            den = den + pltpu.roll(den, shift, 1)
            shift *= 2
        p = jnp.concatenate([e / den for e in es], axis=1)
        pm = jnp.where(diag, jnp.broadcast_to(p, (rows, n)), 0.0).astype(BF16)
        v2 = v_ref[0, i].reshape(n, LANES).astype(BF16)
        o_ref[i * rows:(i + 1) * rows, :] = _dot(pm, v2)

    acc = xs
    for r in range(rows):
        o_r = o_ref[pl.ds(r, nb, stride=rows), :].astype(BF16)
        acc = acc + _dot(o_r, wo_ref[feat(r):feat(r) + LANES, :])
    if final:
        acc = _rms(acc, gf_ref[...])
    xo_ref[...] = acc


def _attn_sample(xs, layer, norm_g, w_q, cache_k, cache_v, w_o, final_g, final):
    n, d = xs.shape
    _, _, m, rows, lanes = cache_k.shape
    nb = SAMPLE_BLOCK
    kvspec = pl.BlockSpec((1, nb, m, rows, lanes), lambda i: (layer, i, 0, 0, 0))
    xspec = pl.BlockSpec((nb, d), lambda i: (i, 0))
    return pl.pallas_call(
        functools.partial(_attn_sample_kernel, final=final),
        grid=(n // nb,),
        in_specs=[xspec, _resident((1, d)), _resident(w_q.shape), kvspec, kvspec,
                  _resident(w_o.shape), _resident((1, d))],
        out_specs=xspec,
        out_shape=jax.ShapeDtypeStruct((n, d), F32),
        scratch_shapes=[pltpu.VMEM((rows * nb, lanes), F32), pltpu.VMEM((nb * rows, lanes), F32)],
        compiler_params=_params(1),
        name="attn_sample",
    )(xs, norm_g.reshape(1, d), w_q, cache_k, cache_v, w_o, final_g.reshape(1, d))


def kernel(x_prompt, x_sample, mem_prompt, state_pool, state_conv, cache_mem_k, cache_mem_v, norm_mix_g, norm_xattn_g, norm_mem_g, w_in_ab, pool_maps, pool_scale, sgu_w, sgu_b, sgu_g, w_out_ab, w_in_c, conv_w, w_out_c, w_q, w_k, w_v, w_o, norm_final_g):
    depth, d = norm_mix_g.shape
    bp = x_prompt.shape[0]
    n_s, dec_seq, _ = x_sample.shape
    assert dec_seq == 1, "sample group is one new token per sequence"
    n_mem = mem_prompt.shape[1]
    w_a, w_b, w_c = pool_scale.shape[-1], sgu_g.shape[-1], conv_w.shape[-1]
    hd = d // XA_HEADS
    assert PAST_LEN % CHUNK == 0, "the sample token must open a spatial-gating chunk"
    assert (hd // LANES) * XA_HEADS == SUBLANES, "one memory token per (8, 128) register tile"

    bf = lambda a: a.astype(BF16)
    w_in_ab_b, w_out_ab_b, maps_b = bf(w_in_ab), bf(w_out_ab), bf(pool_maps)
    w_in_c_b, w_out_c_b = bf(w_in_c), bf(w_out_c)
    w_q_b, w_k_b, w_v_b, w_o_b = bf(w_q), bf(w_k), bf(w_v), bf(w_o)

    k_p, v_p, kt_p, vb_p = _memory_kv(mem_prompt, norm_mem_g, w_k_b, w_v_b)

    cache_k = _lane_tiled(cache_mem_k)
    cache_v = _lane_tiled(cache_mem_v)

    xp = x_prompt
    xs = x_sample.reshape(n_s, d)
    pool_p, pool_s, conv_p, conv_s, vrows_s = [], [], [], [], []
    for i in range(depth):
        j = i // 2
        final = i == depth - 1
        if i % 2 == 0:
            sw = w_b // SGU_GROUPS
            sgbias = jnp.repeat(sgu_b[j].T, sw, axis=1)
            w00 = jnp.repeat(sgu_w[j, :, 0, 0], sw).reshape(1, w_b)
            b0 = jnp.repeat(sgu_b[j, :, 0], sw).reshape(1, w_b)
            xp, tail = _even_prompt(xp, norm_mix_g[i], w_in_ab_b[j], maps_b[j], pool_scale[j],
                                    sgu_w[j], sgbias, sgu_g[j], w_out_ab_b[j])
            pool_p.append(tail[:, POOL_HIST - POOL_CTX:])
            xs, pool_new, vn = _even_sample(xs, jnp.transpose(state_pool[j], (1, 0, 2)),
                                            norm_mix_g[i], w_in_ab_b[j], maps_b[j], pool_scale[j],
                                            w00, b0, sgu_g[j], w_out_ab_b[j], PAST_LEN)
            pool_s.append(jnp.transpose(pool_new, (1, 0, 2)))
            vrows_s.append(vn.reshape(n_s, 1, w_b))
        else:
            xp, tail = _odd_prompt(xp, norm_mix_g[i], w_in_c_b[j], conv_w[j], w_out_c_b[j])
            conv_p.append(tail[:, CONV_HIST - CONV_CTX:])
            xs, conv_new = _odd_sample(xs, state_conv[j].reshape(n_s, CONV_CTX * w_c),
                                       norm_mix_g[i], w_in_c_b[j], conv_w[j], w_out_c_b[j])
            conv_s.append(conv_new.reshape(n_s, CONV_CTX, w_c))
        xp = _attn_prompt(xp, norm_xattn_g[i], w_q_b[i], kt_p[i], vb_p[i], w_o_b[i],
                          norm_final_g, final)
        xs = _attn_sample(xs, i, norm_xattn_g[i], w_q_b[i], cache_k, cache_v, w_o_b[i],
                          norm_final_g, final)

    return (xp, xs.reshape(n_s, 1, d), jnp.stack(pool_p), jnp.stack(pool_s), jnp.stack(conv_p),
            jnp.stack(conv_s), jnp.stack(vrows_s),
            k_p.reshape(depth, bp, n_mem, XA_HEADS, hd), v_p.reshape(depth, bp, n_mem, XA_HEADS, hd))
```

```python
import functools

import jax
import jax.numpy as jnp
from jax import lax
from jax.experimental import pallas as pl
from jax.experimental.pallas import tpu as pltpu

POOL_WINDOWS = (2, 4, 8, 16)
POOL_GROUPS = len(POOL_WINDOWS)
POOL_CTX = max(POOL_WINDOWS) - 1
SGU_GROUPS = 4
CHUNK = 128
CONV_WIDTH = 3
CONV_CTX = CONV_WIDTH - 1
XA_HEADS = 4
EPS = 1e-6
PAST_LEN = 16384

V7X_VMEM_LIMIT_BYTES = 60 * 1024 * 1024
SUBLANES = 8
LANES = 128
POOL_HIST = 16
CONV_HIST = 8

PROMPT_ROWS = 512
SAMPLE_BLOCK = 8

F32 = jnp.float32
BF16 = jnp.bfloat16


def _rms(x, g):
    return x * lax.rsqrt(jnp.mean(x * x, axis=-1, keepdims=True) + EPS) * g


def _silu(x):
    return x * jax.nn.sigmoid(x)


def _dot(a, b):
    return jnp.dot(a, b, preferred_element_type=F32)


def _layer_block(arr, layer):
    index = (layer,) + (0,) * (arr.ndim - 1)
    return pl.BlockSpec((1,) + arr.shape[1:], lambda *_: index, pipeline_mode=pl.Buffered(1))


def _params(n_grid):
    return pltpu.CompilerParams(dimension_semantics=("arbitrary",) * n_grid,
                                vmem_limit_bytes=V7X_VMEM_LIMIT_BYTES)


def _tile_row_feature(r, hd):
    return (r % XA_HEADS) * hd + (r // XA_HEADS) * LANES


def _lane_tiled(cache):
    l, b, m, h, hd = cache.shape
    t = hd // LANES
    return cache.reshape(l, b, m, h, t, LANES).transpose(0, 1, 2, 4, 3, 5).reshape(l, b, m, t * h, LANES)


def _from_lane_tiled(tiled):
    l, b, m, r, lanes = tiled.shape
    t = r // XA_HEADS
    return tiled.reshape(l, b, m, t, XA_HEADS, lanes).transpose(0, 1, 2, 4, 3, 5).reshape(
        l, b, m, XA_HEADS, t * lanes)


def _memkv_kernel(mem_ref, g_ref, wk_ref, wv_ref, k_ref, v_ref, kt_ref, vb_ref):
    d = mem_ref.shape[-1]
    rows = k_ref.shape[3]
    hd = d // XA_HEADS
    mn = _rms(mem_ref[0], g_ref[0]).astype(BF16)
    k = _dot(mn, wk_ref[0])
    v = _dot(mn, wv_ref[0])

    def tile_rows(a):
        pieces = [a[:, _tile_row_feature(r, hd):_tile_row_feature(r, hd) + LANES] for r in range(rows)]
        return jnp.swapaxes(jnp.stack(pieces, axis=0), 0, 1)

    k_ref[0, 0] = tile_rows(k)
    v_ref[0, 0] = tile_rows(v)
    kt_ref[0, 0] = k.T.astype(BF16)
    vb_ref[0, 0] = v.astype(BF16)


def _memory_kv(mem, norm_g, w_k, w_v):
    depth, _, d = norm_g.shape
    b, m, _ = mem.shape
    rows = d // LANES
    wspec = pl.BlockSpec((1, d, d), lambda l, i: (l, 0, 0))
    tiled = pl.BlockSpec((1, 1, m, rows, LANES), lambda l, i: (l, i, 0, 0, 0))
    return pl.pallas_call(
        _memkv_kernel,
        grid=(depth, b),
        in_specs=[pl.BlockSpec((1, m, d), lambda l, i: (i, 0, 0)),
                  pl.BlockSpec((1, 1, d), lambda l, i: (l, 0, 0)),
                  wspec, wspec],
        out_specs=[tiled, tiled,
                   pl.BlockSpec((1, 1, d, m), lambda l, i: (l, i, 0, 0)),
                   pl.BlockSpec((1, 1, m, d), lambda l, i: (l, i, 0, 0))],
        out_shape=[jax.ShapeDtypeStruct((depth, b, m, rows, LANES), F32),
                   jax.ShapeDtypeStruct((depth, b, m, rows, LANES), F32),
                   jax.ShapeDtypeStruct((depth, b, d, m), BF16),
                   jax.ShapeDtypeStruct((depth, b, m, d), BF16)],
        compiler_params=_params(2),
        name="memory_kv",
    )(mem, norm_g, w_k, w_v)


def _even_prompt_kernel(x_ref, g_ref, w_in_ref, maps_ref, pscale_ref, sgw_ref, sgbias_ref, sgg_ref,
                        w_out_ref, xo_ref, tail_ref, z_ref, y_ref, hist_ref):
    rows, w_a = y_ref.shape[0], pscale_ref.shape[-1]
    w_b = sgg_ref.shape[-1]
    gw = w_a // POOL_GROUPS
    sw = w_b // SGU_GROUPS
    s = pl.program_id(1)

    @pl.when(s == 0)
    def _():
        hist_ref[...] = jnp.zeros_like(hist_ref)

    x = x_ref[0]
    h = _rms(x, g_ref[0]).astype(BF16)
    z_ref[...] = _dot(h, w_in_ref[0])

    pos = s * rows + lax.broadcasted_iota(jnp.int32, (rows, gw), 0)
    for g, w in enumerate(POOL_WINDOWS):
        cols = slice(g * gw, (g + 1) * gw)
        xa = z_ref[:, cols]
        acc = jnp.concatenate([hist_ref[:, cols], xa], axis=0)
        shift = 1
        while shift < w:
            acc = acc + pltpu.roll(acc, shift, 0)
            shift *= 2
        cnt = jnp.minimum(pos + 1, w).astype(F32)
        pooled = acc[POOL_HIST:] / cnt - xa
        ya = _dot(pooled.astype(BF16), maps_ref[0, g]) * pscale_ref[0, :, cols]
        ya = ya * _silu(z_ref[:, w_a + g * gw:w_a + (g + 1) * gw])
        y_ref[:, cols] = ya.astype(BF16)
    tail = z_ref[rows - POOL_HIST:, 0:w_a]
    hist_ref[...] = tail
    tail_ref[0] = tail

    vn = _rms(z_ref[:, 2 * w_a + w_b:2 * w_a + 2 * w_b], sgg_ref[0]).astype(BF16)
    causal = (lax.broadcasted_iota(jnp.int32, (CHUNK, CHUNK), 0)
              >= lax.broadcasted_iota(jnp.int32, (CHUNK, CHUNK), 1))
    for g in range(SGU_GROUPS):
        cols = slice(g * sw, (g + 1) * sw)
        wg = jnp.where(causal, sgw_ref[0, g], 0.0).astype(BF16)
        for c in range(rows // CHUNK):
            rs = slice(c * CHUNK, (c + 1) * CHUNK)
            mixed = _dot(wg, vn[rs, cols]) + sgbias_ref[0, :, cols]
            u = z_ref[rs, 2 * w_a + g * sw:2 * w_a + (g + 1) * sw]
            gb = z_ref[rs, 2 * w_a + 2 * w_b + g * sw:2 * w_a + 2 * w_b + (g + 1) * sw]
            y_ref[rs, w_a + g * sw:w_a + (g + 1) * sw] = (u * mixed * _silu(gb)).astype(BF16)

    xo_ref[0] = x + _dot(y_ref[...], w_out_ref[0])


def _even_prompt(x, layer, j, norm_g, w_in, maps, pscale, sgw, sgbias, sgg, w_out):
    b, seq, d = x.shape
    w_a, w_b = pscale.shape[-1], sgg.shape[-1]
    rows = PROMPT_ROWS
    xspec = pl.BlockSpec((1, rows, d), lambda i, s: (i, s, 0))
    return pl.pallas_call(
        _even_prompt_kernel,
        grid=(b, seq // rows),
        in_specs=[xspec, _layer_block(norm_g, layer), _layer_block(w_in, j), _layer_block(maps, j),
                  _layer_block(pscale, j), _layer_block(sgw, j), _layer_block(sgbias, j),
                  _layer_block(sgg, j), _layer_block(w_out, j)],
        out_specs=[xspec, pl.BlockSpec((1, POOL_HIST, w_a), lambda i, s: (i, 0, 0))],
        out_shape=[jax.ShapeDtypeStruct(x.shape, F32),
                   jax.ShapeDtypeStruct((b, POOL_HIST, w_a), F32)],
        scratch_shapes=[pltpu.VMEM((rows, w_in.shape[-1]), F32),
                        pltpu.VMEM((rows, w_a + w_b), BF16),
                        pltpu.VMEM((POOL_HIST, w_a), F32)],
        compiler_params=_params(2),
        name="even_prompt",
    )(x, norm_g, w_in, maps, pscale, sgw, sgbias, sgg, w_out)


def _odd_prompt_kernel(x_ref, g_ref, w_in_ref, cw_ref, w_out_ref, xo_ref, tail_ref,
                       z_ref, y_ref, hist_ref, *, col_slab):
    rows, w_c = y_ref.shape
    s = pl.program_id(1)

    @pl.when(s == 0)
    def _():
        hist_ref[...] = jnp.zeros_like(hist_ref)

    x = x_ref[0]
    h = _rms(x, g_ref[0]).astype(BF16)
    for n, c0 in enumerate(range(0, w_c, col_slab)):
        cols = slice(c0, c0 + col_slab)
        zb = z_ref.at[n % 2]
        for part in range(4):
            zb[part] = _dot(h, w_in_ref[0, :, part * w_c + c0:part * w_c + c0 + col_slab])
        e = zb[1] * zb[2]
        ext = jnp.concatenate([hist_ref[:, cols], e], axis=0)
        y = (cw_ref[0, 0:1, cols] * pltpu.roll(ext, 2, 0)[CONV_HIST:]
             + cw_ref[0, 1:2, cols] * pltpu.roll(ext, 1, 0)[CONV_HIST:]
             + cw_ref[0, 2:3, cols] * e)
        tail = e[rows - CONV_HIST:]
        hist_ref[:, cols] = tail
        tail_ref[0, :, cols] = tail
        y_ref[:, cols] = (zb[0] * y * _silu(zb[3])).astype(BF16)
    xo_ref[0] = x + _dot(y_ref[...], w_out_ref[0])


def _odd_prompt(x, layer, j, norm_g, w_in, conv_w, w_out):
    b, seq, d = x.shape
    w_c = conv_w.shape[-1]
    rows = PROMPT_ROWS
    col_slab = 512
    xspec = pl.BlockSpec((1, rows, d), lambda i, s: (i, s, 0))
    return pl.pallas_call(
        functools.partial(_odd_prompt_kernel, col_slab=col_slab),
        grid=(b, seq // rows),
        in_specs=[xspec, _layer_block(norm_g, layer), _layer_block(w_in, j), _layer_block(conv_w, j),
                  _layer_block(w_out, j)],
        out_specs=[xspec, pl.BlockSpec((1, CONV_HIST, w_c), lambda i, s: (i, 0, 0))],
        out_shape=[jax.ShapeDtypeStruct(x.shape, F32),
                   jax.ShapeDtypeStruct((b, CONV_HIST, w_c), F32)],
        scratch_shapes=[pltpu.VMEM((2, 4, rows, col_slab), F32),
                        pltpu.VMEM((rows, w_c), BF16),
                        pltpu.VMEM((CONV_HIST, w_c), F32)],
        compiler_params=_params(2),
        name="odd_prompt",
    )(x, norm_g, w_in, conv_w, w_out)


def _attn_prompt_kernel(x_ref, g_ref, wq_ref, kt_ref, v_ref, wo_ref, gf_ref, xo_ref, o_ref, *, final):
    d = x_ref.shape[-1]
    hd = d // XA_HEADS
    x = x_ref[0]
    h = _rms(x, g_ref[0]).astype(BF16)
    q = _dot(h, wq_ref[0]).astype(BF16)
    for hh in range(XA_HEADS):
        cols = slice(hh * hd, (hh + 1) * hd)
        s = _dot(q[:, cols], kt_ref[0, 0, cols, :]) * (hd ** -0.5)
        e = jnp.exp(s - jnp.max(s, axis=-1, keepdims=True))
        p = e / jnp.sum(e, axis=-1, keepdims=True)
        o_ref[:, cols] = _dot(p.astype(BF16), v_ref[0, 0, :, cols]).astype(BF16)
    xn = x + _dot(o_ref[...], wo_ref[0])
    if final:
        xn = _rms(xn, gf_ref[...])
    xo_ref[0] = xn


def _attn_prompt(x, layer, norm_g, w_q, kt, vb, w_o, final_g, final):
    b, seq, d = x.shape
    m = vb.shape[2]
    rows = PROMPT_ROWS
    xspec = pl.BlockSpec((1, rows, d), lambda i, s: (i, s, 0))
    return pl.pallas_call(
        functools.partial(_attn_prompt_kernel, final=final),
        grid=(b, seq // rows),
        in_specs=[xspec, _layer_block(norm_g, layer), _layer_block(w_q, layer),
                  pl.BlockSpec((1, 1, d, m), lambda i, s: (layer, i, 0, 0)),
                  pl.BlockSpec((1, 1, m, d), lambda i, s: (layer, i, 0, 0)),
                  _layer_block(w_o, layer),
                  pl.BlockSpec((1, d), lambda i, s: (0, 0), pipeline_mode=pl.Buffered(1))],
        out_specs=xspec,
        out_shape=jax.ShapeDtypeStruct(x.shape, F32),
        scratch_shapes=[pltpu.VMEM((rows, d), BF16)],
        compiler_params=_params(2),
        name="attn_prompt",
    )(x, norm_g, w_q, kt, vb, w_o, final_g)


def _even_sample_kernel(xs_ref, st_ref, g_ref, w_in_ref, maps_ref, pscale_ref, w00_ref, b0_ref,
                        sgg_ref, w_out_ref, xo_ref, pool_ref, vn_ref, y_ref, *, pos0):
    w_a, w_b = pscale_ref.shape[-1], sgg_ref.shape[-1]
    gw = w_a // POOL_GROUPS
    xs = xs_ref[...]
    h = _rms(xs, g_ref[0]).astype(BF16)
    xa = _dot(h, w_in_ref[0, :, 0:w_a])
    ga = _dot(h, w_in_ref[0, :, w_a:2 * w_a])
    u = _dot(h, w_in_ref[0, :, 2 * w_a:2 * w_a + w_b])
    v = _dot(h, w_in_ref[0, :, 2 * w_a + w_b:2 * w_a + 2 * w_b])
    gb = _dot(h, w_in_ref[0, :, 2 * w_a + 2 * w_b:])

    for g, w in enumerate(POOL_WINDOWS):
        cols = slice(g * gw, (g + 1) * gw)
        acc = xa[:, cols]
        for k in range(POOL_CTX - (w - 1), POOL_CTX):
            acc = acc + st_ref[0, k, :, cols]
        pooled = acc / float(min(pos0 + 1, w)) - xa[:, cols]
        ya = _dot(pooled.astype(BF16), maps_ref[0, g]) * pscale_ref[0, :, cols] * _silu(ga[:, cols])
        y_ref[:, cols] = ya.astype(BF16)
    pool_ref[0, 0:POOL_CTX - 1] = st_ref[0, 1:POOL_CTX]
    pool_ref[0, POOL_CTX - 1] = xa

    vn = _rms(v, sgg_ref[0])
    vn_ref[0] = vn
    mixed = w00_ref[0] * vn + b0_ref[0]
    y_ref[:, w_a:] = (u * mixed * _silu(gb)).astype(BF16)
    xo_ref[...] = xs + _dot(y_ref[...], w_out_ref[0])


def _even_sample(xs, layer, j, state, norm_g, w_in, maps, pscale, w00, b0, sgg, w_out, pos0):
    n, d = xs.shape
    n_even, ctx, _, w_a = state.shape
    w_b = sgg.shape[-1]
    whole = pl.BlockSpec((n, d), lambda i: (0, 0))
    return pl.pallas_call(
        functools.partial(_even_sample_kernel, pos0=pos0),
        grid=(1,),
        in_specs=[whole, _layer_block(state, j), _layer_block(norm_g, layer), _layer_block(w_in, j),
                  _layer_block(maps, j), _layer_block(pscale, j), _layer_block(w00, j),
                  _layer_block(b0, j), _layer_block(sgg, j), _layer_block(w_out, j)],
        out_specs=[whole, pl.BlockSpec((1, ctx, n, w_a), lambda i: (0, 0, 0, 0)),
                   pl.BlockSpec((1, n, w_b), lambda i: (0, 0, 0))],
        out_shape=[jax.ShapeDtypeStruct((n, d), F32),
                   jax.ShapeDtypeStruct((1, ctx, n, w_a), F32),
                   jax.ShapeDtypeStruct((1, n, w_b), F32)],
        scratch_shapes=[pltpu.VMEM((n, w_a + w_b), BF16)],
        compiler_params=_params(1),
        name="even_sample",
    )(xs, state, norm_g, w_in, maps, pscale, w00, b0, sgg, w_out)


def _odd_sample_kernel(xs_ref, st_ref, g_ref, w_in_ref, cw_ref, w_out_ref, xo_ref, conv_ref, y_ref):
    w_c = cw_ref.shape[-1]
    xs = xs_ref[...]
    h = _rms(xs, g_ref[0]).astype(BF16)
    bg = _dot(h, w_in_ref[0, :, 0:w_c])
    cg = _dot(h, w_in_ref[0, :, w_c:2 * w_c])
    xc = _dot(h, w_in_ref[0, :, 2 * w_c:3 * w_c])
    gate = _dot(h, w_in_ref[0, :, 3 * w_c:])
    e = cg * xc
    y = (cw_ref[0, 0:1, :] * st_ref[:, 0:w_c] + cw_ref[0, 1:2, :] * st_ref[:, w_c:]
         + cw_ref[0, 2:3, :] * e)
    conv_ref[:, 0:w_c] = st_ref[:, w_c:]
    conv_ref[:, w_c:] = e
    y_ref[...] = (bg * y * _silu(gate)).astype(BF16)
    xo_ref[...] = xs + _dot(y_ref[...], w_out_ref[0])


def _odd_sample(xs, layer, j, state, norm_g, w_in, conv_w, w_out):
    n, d = xs.shape
    w_c = conv_w.shape[-1]
    whole = pl.BlockSpec((n, d), lambda i: (0, 0))
    stspec = pl.BlockSpec((n, CONV_CTX * w_c), lambda i: (0, 0))
    return pl.pallas_call(
        _odd_sample_kernel,
        grid=(1,),
        in_specs=[whole, stspec, _layer_block(norm_g, layer), _layer_block(w_in, j),
                  _layer_block(conv_w, j), _layer_block(w_out, j)],
        out_specs=[whole, stspec],
        out_shape=[jax.ShapeDtypeStruct((n, d), F32),
                   jax.ShapeDtypeStruct((n, CONV_CTX * w_c), F32)],
        scratch_shapes=[pltpu.VMEM((n, w_c), BF16)],
        compiler_params=_params(1),
        name="odd_sample",
    )(xs, state, norm_g, w_in, conv_w, w_out)


def _attn_sample_kernel(xs_ref, g_ref, wq_ref, k_ref, v_ref, wo_ref, gf_ref, xo_ref,
                        q_ref, t_ref, o_ref, *, final):
    nb, d = xs_ref.shape
    m, rows = k_ref.shape[2], k_ref.shape[3]
    hd = d // XA_HEADS
    n = m * rows
    feat = functools.partial(_tile_row_feature, hd=hd)

    xs = xs_ref[...]
    h = _rms(xs, g_ref[0]).astype(BF16)
    q = _dot(h, wq_ref[0])
    for r in range(rows):
        q_ref[r * nb:(r + 1) * nb, :] = q[:, feat(r):feat(r) + LANES]

    lane = lax.broadcasted_iota(jnp.int32, (rows, n), 1)
    diag = lane % rows == lax.broadcasted_iota(jnp.int32, (rows, n), 0)

    for i in range(nb):
        qt = q_ref[pl.ds(i, rows, stride=nb), :].astype(BF16)
        k2 = k_ref[0, i].reshape(n, LANES).astype(BF16)
        s_all = lax.dot_general(qt, k2, (((1,), (1,)), ((), ())), preferred_element_type=F32)
        t_ref[i:i + 1, :] = jnp.sum(jnp.where(diag, s_all, 0.0), axis=0, keepdims=True)

    t = t_ref[...]
    u = t + pltpu.roll(t, XA_HEADS, 1)
    upper = lax.broadcasted_iota(jnp.int32, (nb, n), 1) % rows >= XA_HEADS
    s = jnp.where(upper, u, pltpu.roll(u, n - XA_HEADS, 1)) * (hd ** -0.5)
    parts = [s[:, j * LANES:(j + 1) * LANES] for j in range(n // LANES)]
    mx = functools.reduce(jnp.maximum, parts)
    shift = rows
    while shift < LANES:
        mx = jnp.maximum(mx, pltpu.roll(mx, shift, 1))
        shift *= 2
    es = [jnp.exp(part - mx) for part in parts]
    den = functools.reduce(jnp.add, es)
    shift = rows
    while shift < LANES:
        den = den + pltpu.roll(den, shift, 1)
        shift *= 2
    p = jnp.concatenate([e / den for e in es], axis=1)

    for i in range(nb):
        pm = jnp.where(diag, jnp.broadcast_to(p[i:i + 1, :], (rows, n)), 0.0).astype(BF16)
        v2 = v_ref[0, i].reshape(n, LANES).astype(BF16)
        o_ref[i * rows:(i + 1) * rows, :] = _dot(pm, v2)

    acc = xs
    for r in range(rows):
        o_r = o_ref[pl.ds(r, nb, stride=rows), :].astype(BF16)
        acc = acc + _dot(o_r, wo_ref[0, feat(r):feat(r) + LANES, :])
    if final:
        acc = _rms(acc, gf_ref[...])
    xo_ref[...] = acc


def _attn_sample(xs, layer, norm_g, w_q, cache_k, cache_v, w_o, final_g, final):
    n, d = xs.shape
    _, _, m, rows, lanes = cache_k.shape
    nb = SAMPLE_BLOCK
    kvspec = pl.BlockSpec((1, nb, m, rows, lanes), lambda i: (layer, i, 0, 0, 0))
    xspec = pl.BlockSpec((nb, d), lambda i: (i, 0))
    return pl.pallas_call(
        functools.partial(_attn_sample_kernel, final=final),
        grid=(n // nb,),
        in_specs=[xspec, _layer_block(norm_g, layer), _layer_block(w_q, layer), kvspec, kvspec,
                  _layer_block(w_o, layer),
                  pl.BlockSpec((1, d), lambda i: (0, 0), pipeline_mode=pl.Buffered(1))],
        out_specs=xspec,
        out_shape=jax.ShapeDtypeStruct((n, d), F32),
        scratch_shapes=[pltpu.VMEM((rows * nb, lanes), F32), pltpu.VMEM((nb, m * rows), F32),
                        pltpu.VMEM((nb * rows, lanes), F32)],
        compiler_params=_params(1),
        name="attn_sample",
    )(xs, norm_g, w_q, cache_k, cache_v, w_o, final_g)


def kernel(x_prompt, x_sample, mem_prompt, state_pool, state_conv, cache_mem_k, cache_mem_v, norm_mix_g, norm_xattn_g, norm_mem_g, w_in_ab, pool_maps, pool_scale, sgu_w, sgu_b, sgu_g, w_out_ab, w_in_c, conv_w, w_out_c, w_q, w_k, w_v, w_o, norm_final_g):
    depth, d = norm_mix_g.shape
    n_s, dec_seq, _ = x_sample.shape
    n_even, n_odd = pool_scale.shape[0], conv_w.shape[0]
    w_a, w_b, w_c = pool_scale.shape[-1], sgu_g.shape[-1], conv_w.shape[-1]
    hd = d // XA_HEADS
    assert dec_seq == 1, "sample group is one new token per sequence"
    assert PAST_LEN % CHUNK == 0, "the sample token must open a spatial-gating chunk"
    assert (hd // LANES) * XA_HEADS == SUBLANES, "one memory token per (8, 128) register tile"

    bf = lambda a: a.astype(BF16)
    w_in_ab_b, w_out_ab_b, maps_b = bf(w_in_ab), bf(w_out_ab), bf(pool_maps)
    w_in_c_b, w_out_c_b = bf(w_in_c), bf(w_out_c)
    w_q_b, w_k_b, w_v_b, w_o_b = bf(w_q), bf(w_k), bf(w_v), bf(w_o)
    row = lambda a: a.reshape(a.shape[0], 1, a.shape[-1])
    g_mix, g_xattn, g_mem = row(norm_mix_g), row(norm_xattn_g), row(norm_mem_g)
    pscale, sgg = row(pool_scale), row(sgu_g)
    g_final = norm_final_g.reshape(1, d)
    sw = w_b // SGU_GROUPS
    sgbias = jnp.repeat(jnp.swapaxes(sgu_b, 1, 2), sw, axis=2)
    w00 = row(jnp.repeat(sgu_w[:, :, 0, 0], sw, axis=1))
    b0 = row(jnp.repeat(sgu_b[:, :, 0], sw, axis=1))

    k_p, v_p, kt_p, vb_p = _memory_kv(mem_prompt, g_mem, w_k_b, w_v_b)
    cache_k = _lane_tiled(cache_mem_k)
    cache_v = _lane_tiled(cache_mem_v)
    pool_state = jnp.swapaxes(state_pool, 1, 2)
    conv_state = state_conv.reshape(n_odd, n_s, CONV_CTX * w_c)

    xp = x_prompt
    xs = x_sample.reshape(n_s, d)
    pool_p, pool_s, conv_p, conv_s, vrows_s = [], [], [], [], []
    for i in range(depth):
        j = i // 2
        final = i == depth - 1
        if i % 2 == 0:
            xp, tail = _even_prompt(xp, i, j, g_mix, w_in_ab_b, maps_b, pscale, sgu_w, sgbias, sgg,
                                    w_out_ab_b)
            pool_p.append(tail[:, POOL_HIST - POOL_CTX:])
            xs, pool_new, vn = _even_sample(xs, i, j, pool_state, g_mix, w_in_ab_b, maps_b, pscale,
                                            w00, b0, sgg, w_out_ab_b, PAST_LEN)
            pool_s.append(pool_new)
            vrows_s.append(vn)
        else:
            xp, tail = _odd_prompt(xp, i, j, g_mix, w_in_c_b, conv_w, w_out_c_b)
            conv_p.append(tail[:, CONV_HIST - CONV_CTX:])
            xs, conv_new = _odd_sample(xs, i, j, conv_state[j], g_mix, w_in_c_b, conv_w, w_out_c_b)
            conv_s.append(conv_new.reshape(n_s, CONV_CTX, w_c))
        xp = _attn_prompt(xp, i, g_xattn, w_q_b, kt_p, vb_p, w_o_b, g_final, final)
        xs = _attn_sample(xs, i, g_xattn, w_q_b, cache_k, cache_v, w_o_b, g_final, final)

    new_pool_s = jnp.swapaxes(jnp.concatenate(pool_s, axis=0), 1, 2)
    new_vrows = jnp.concatenate(vrows_s, axis=0).reshape(n_even, n_s, 1, w_b)
    return (xp, xs.reshape(n_s, 1, d), jnp.stack(pool_p), new_pool_s, jnp.stack(conv_p),
            jnp.stack(conv_s), new_vrows, _from_lane_tiled(k_p), _from_lane_tiled(v_p))
```

```python
import functools

import jax
import jax.numpy as jnp
from jax import lax
from jax.experimental import pallas as pl
from jax.experimental.pallas import tpu as pltpu

POOL_WINDOWS = (2, 4, 8, 16)
POOL_GROUPS = len(POOL_WINDOWS)
POOL_CTX = max(POOL_WINDOWS) - 1
SGU_GROUPS = 4
CHUNK = 128
CONV_WIDTH = 3
CONV_CTX = CONV_WIDTH - 1
XA_HEADS = 4
EPS = 1e-6
PAST_LEN = 16384

V7X_VMEM_LIMIT_BYTES = 60 * 1024 * 1024
SUBLANES = 8
LANES = 128
POOL_HIST = 16
CONV_HIST = 8

PROMPT_ROWS = 512

F32 = jnp.float32
BF16 = jnp.bfloat16


def _rms(x, g):
    return x * lax.rsqrt(jnp.mean(x * x, axis=-1, keepdims=True) + EPS) * g


def _silu(x):
    return x * jax.nn.sigmoid(x)


def _dot(a, b):
    return jnp.dot(a, b, preferred_element_type=F32)


def _layer_block(arr, layer):
    index = (layer,) + (0,) * (arr.ndim - 1)
    return pl.BlockSpec((1,) + arr.shape[1:], lambda *_: index, pipeline_mode=pl.Buffered(1))


def _params(n_grid):
    return pltpu.CompilerParams(dimension_semantics=("arbitrary",) * n_grid,
                                vmem_limit_bytes=V7X_VMEM_LIMIT_BYTES)


def _tile_row_feature(r, hd):
    return (r % XA_HEADS) * hd + (r // XA_HEADS) * LANES


def _lane_tiled(cache):
    l, b, m, h, hd = cache.shape
    t = hd // LANES
    return cache.reshape(l, b, m, h, t, LANES).transpose(0, 1, 2, 4, 3, 5).reshape(l, b, m, t * h, LANES)


def _from_lane_tiled(tiled):
    l, b, m, r, lanes = tiled.shape
    t = r // XA_HEADS
    return tiled.reshape(l, b, m, t, XA_HEADS, lanes).transpose(0, 1, 2, 4, 3, 5).reshape(
        l, b, m, XA_HEADS, t * lanes)


def _to_tile_rows(a, rows):
    hd = a.shape[1] // XA_HEADS
    pieces = [a[:, _tile_row_feature(r, hd):_tile_row_feature(r, hd) + LANES] for r in range(rows)]
    return jnp.swapaxes(jnp.stack(pieces, axis=0), 0, 1)


def _memkv_kernel(mem_ref, g_ref, wk_ref, wv_ref, k_ref, v_ref, kt_ref, vb_ref):
    rows = k_ref.shape[3]
    mn = _rms(mem_ref[0], g_ref[0]).astype(BF16)
    k = _dot(mn, wk_ref[0])
    v = _dot(mn, wv_ref[0])
    k_ref[0, 0] = _to_tile_rows(k, rows)
    v_ref[0, 0] = _to_tile_rows(v, rows)
    kt_ref[0, 0] = k.T.astype(BF16)
    vb_ref[0, 0] = v.astype(BF16)


def _memory_kv(mem, norm_g, w_k, w_v):
    depth, _, d = norm_g.shape
    b, m, _ = mem.shape
    rows = d // LANES
    wspec = pl.BlockSpec((1, d, d), lambda l, i: (l, 0, 0))
    tiled = pl.BlockSpec((1, 1, m, rows, LANES), lambda l, i: (l, i, 0, 0, 0))
    return pl.pallas_call(
        _memkv_kernel,
        grid=(depth, b),
        in_specs=[pl.BlockSpec((1, m, d), lambda l, i: (i, 0, 0)),
                  pl.BlockSpec((1, 1, d), lambda l, i: (l, 0, 0)),
                  wspec, wspec],
        out_specs=[tiled, tiled,
                   pl.BlockSpec((1, 1, d, m), lambda l, i: (l, i, 0, 0)),
                   pl.BlockSpec((1, 1, m, d), lambda l, i: (l, i, 0, 0))],
        out_shape=[jax.ShapeDtypeStruct((depth, b, m, rows, LANES), F32),
                   jax.ShapeDtypeStruct((depth, b, m, rows, LANES), F32),
                   jax.ShapeDtypeStruct((depth, b, d, m), BF16),
                   jax.ShapeDtypeStruct((depth, b, m, d), BF16)],
        compiler_params=_params(2),
        name="memory_kv",
    )(mem, norm_g, w_k, w_v)


def _sample_attention(q_ref, k_ref, v_ref, o_ref):
    n, rows, lanes = q_ref.shape
    hd = rows * lanes // XA_HEADS
    for i in range(n):
        qt = q_ref[i] * (hd ** -0.5)
        part = jnp.sum(k_ref[0, i] * qt[None], axis=-1, keepdims=True)
        s = part + pltpu.roll(part, XA_HEADS, 1)
        e = jnp.exp(s - jnp.max(s, axis=0, keepdims=True))
        den = jnp.sum(e, axis=0)
        o_ref[i] = jnp.sum(e * v_ref[0, i], axis=0) / den


def _sample_attention_specs(q_tiles, cache_k, layer, n_steps):
    n, rows, lanes = q_tiles.shape
    m = cache_k.shape[2]
    per_step = n // (n_steps[0] * n_steps[1])
    step = lambda i, s: i * n_steps[1] + s
    qspec = pl.BlockSpec((per_step, rows, lanes), lambda i, s: (step(i, s), 0, 0))
    kvspec = pl.BlockSpec((1, per_step, m, rows, lanes), lambda i, s: (layer, step(i, s), 0, 0, 0))
    return qspec, kvspec


def _even_prompt_kernel(x_ref, g_ref, w_in_ref, maps_ref, pscale_ref, sgw_ref, sgbias_ref, sgg_ref,
                        w_out_ref, sq_ref, sk_ref, sv_ref, xo_ref, tail_ref, so_ref,
                        z_ref, y_ref, hist_ref):
    rows, w_a = y_ref.shape[0], pscale_ref.shape[-1]
    w_b = sgg_ref.shape[-1]
    gw = w_a // POOL_GROUPS
    sw = w_b // SGU_GROUPS
    s = pl.program_id(1)

    @pl.when(s == 0)
    def _():
        hist_ref[...] = jnp.zeros_like(hist_ref)

    x = x_ref[0]
    h = _rms(x, g_ref[0]).astype(BF16)
    z_ref[...] = _dot(h, w_in_ref[0])

    pos = s * rows + lax.broadcasted_iota(jnp.int32, (rows, gw), 0)
    for g, w in enumerate(POOL_WINDOWS):
        cols = slice(g * gw, (g + 1) * gw)
        xa = z_ref[:, cols]
        acc = jnp.concatenate([hist_ref[:, cols], xa], axis=0)
        shift = 1
        while shift < w:
            acc = acc + pltpu.roll(acc, shift, 0)
            shift *= 2
        cnt = jnp.minimum(pos + 1, w).astype(F32)
        pooled = acc[POOL_HIST:] / cnt - xa
        ya = _dot(pooled.astype(BF16), maps_ref[0, g]) * pscale_ref[0, :, cols]
        ya = ya * _silu(z_ref[:, w_a + g * gw:w_a + (g + 1) * gw])
        y_ref[:, cols] = ya.astype(BF16)
    tail = z_ref[rows - POOL_HIST:, 0:w_a]
    hist_ref[...] = tail
    tail_ref[0] = tail

    vn = _rms(z_ref[:, 2 * w_a + w_b:2 * w_a + 2 * w_b], sgg_ref[0]).astype(BF16)
    causal = (lax.broadcasted_iota(jnp.int32, (CHUNK, CHUNK), 0)
              >= lax.broadcasted_iota(jnp.int32, (CHUNK, CHUNK), 1))
    for g in range(SGU_GROUPS):
        cols = slice(g * sw, (g + 1) * sw)
        wg = jnp.where(causal, sgw_ref[0, g], 0.0).astype(BF16)
        for c in range(rows // CHUNK):
            rs = slice(c * CHUNK, (c + 1) * CHUNK)
            mixed = _dot(wg, vn[rs, cols]) + sgbias_ref[0, :, cols]
            u = z_ref[rs, 2 * w_a + g * sw:2 * w_a + (g + 1) * sw]
            gb = z_ref[rs, 2 * w_a + 2 * w_b + g * sw:2 * w_a + 2 * w_b + (g + 1) * sw]
            y_ref[rs, w_a + g * sw:w_a + (g + 1) * sw] = (u * mixed * _silu(gb)).astype(BF16)

    xo_ref[0] = x + _dot(y_ref[...], w_out_ref[0])
    _sample_attention(sq_ref, sk_ref, sv_ref, so_ref)


def _even_prompt(x, layer, j, norm_g, w_in, maps, pscale, sgw, sgbias, sgg, w_out, sq, cache_k, cache_v):
    b, seq, d = x.shape
    w_a, w_b = pscale.shape[-1], sgg.shape[-1]
    rows = PROMPT_ROWS
    xspec = pl.BlockSpec((1, rows, d), lambda i, s: (i, s, 0))
    qspec, kvspec = _sample_attention_specs(sq, cache_k, layer, (b, seq // rows))
    return pl.pallas_call(
        _even_prompt_kernel,
        grid=(b, seq // rows),
        in_specs=[xspec, _layer_block(norm_g, layer), _layer_block(w_in, j), _layer_block(maps, j),
                  _layer_block(pscale, j), _layer_block(sgw, j), _layer_block(sgbias, j),
                  _layer_block(sgg, j), _layer_block(w_out, j), qspec, kvspec, kvspec],
        out_specs=[xspec, pl.BlockSpec((1, POOL_HIST, w_a), lambda i, s: (i, 0, 0)), qspec],
        out_shape=[jax.ShapeDtypeStruct(x.shape, F32),
                   jax.ShapeDtypeStruct((b, POOL_HIST, w_a), F32),
                   jax.ShapeDtypeStruct(sq.shape, F32)],
        scratch_shapes=[pltpu.VMEM((rows, w_in.shape[-1]), F32),
                        pltpu.VMEM((rows, w_a + w_b), BF16),
                        pltpu.VMEM((POOL_HIST, w_a), F32)],
        compiler_params=_params(2),
        name="even_prompt",
    )(x, norm_g, w_in, maps, pscale, sgw, sgbias, sgg, w_out, sq, cache_k, cache_v)


def _odd_prompt_kernel(x_ref, g_ref, w_in_ref, cw_ref, w_out_ref, sq_ref, sk_ref, sv_ref,
                       xo_ref, tail_ref, so_ref, z_ref, y_ref, hist_ref, *, col_slab):
    rows, w_c = y_ref.shape
    s = pl.program_id(1)

    @pl.when(s == 0)
    def _():
        hist_ref[...] = jnp.zeros_like(hist_ref)

    x = x_ref[0]
    h = _rms(x, g_ref[0]).astype(BF16)
    for n, c0 in enumerate(range(0, w_c, col_slab)):
        cols = slice(c0, c0 + col_slab)
        zb = z_ref.at[n % 2]
        for part in range(4):
            zb[part] = _dot(h, w_in_ref[0, :, part * w_c + c0:part * w_c + c0 + col_slab])
        e = zb[1] * zb[2]
        ext = jnp.concatenate([hist_ref[:, cols], e], axis=0)
        y = (cw_ref[0, 0:1, cols] * pltpu.roll(ext, 2, 0)[CONV_HIST:]
             + cw_ref[0, 1:2, cols] * pltpu.roll(ext, 1, 0)[CONV_HIST:]
             + cw_ref[0, 2:3, cols] * e)
        tail = e[rows - CONV_HIST:]
        hist_ref[:, cols] = tail
        tail_ref[0, :, cols] = tail
        y_ref[:, cols] = (zb[0] * y * _silu(zb[3])).astype(BF16)
    xo_ref[0] = x + _dot(y_ref[...], w_out_ref[0])
    _sample_attention(sq_ref, sk_ref, sv_ref, so_ref)


def _odd_prompt(x, layer, j, norm_g, w_in, conv_w, w_out, sq, cache_k, cache_v):
    b, seq, d = x.shape
    w_c = conv_w.shape[-1]
    rows = PROMPT_ROWS
    col_slab = 512
    xspec = pl.BlockSpec((1, rows, d), lambda i, s: (i, s, 0))
    qspec, kvspec = _sample_attention_specs(sq, cache_k, layer, (b, seq // rows))
    return pl.pallas_call(
        functools.partial(_odd_prompt_kernel, col_slab=col_slab),
        grid=(b, seq // rows),
        in_specs=[xspec, _layer_block(norm_g, layer), _layer_block(w_in, j), _layer_block(conv_w, j),
                  _layer_block(w_out, j), qspec, kvspec, kvspec],
        out_specs=[xspec, pl.BlockSpec((1, CONV_HIST, w_c), lambda i, s: (i, 0, 0)), qspec],
        out_shape=[jax.ShapeDtypeStruct(x.shape, F32),
                   jax.ShapeDtypeStruct((b, CONV_HIST, w_c), F32),
                   jax.ShapeDtypeStruct(sq.shape, F32)],
        scratch_shapes=[pltpu.VMEM((2, 4, rows, col_slab), F32),
                        pltpu.VMEM((rows, w_c), BF16),
                        pltpu.VMEM((CONV_HIST, w_c), F32)],
        compiler_params=_params(2),
        name="odd_prompt",
    )(x, norm_g, w_in, conv_w, w_out, sq, cache_k, cache_v)


def _attn_prompt_kernel(x_ref, g_ref, wq_ref, kt_ref, v_ref, wo_ref, gf_ref, xo_ref, o_ref, *, final):
    d = x_ref.shape[-1]
    hd = d // XA_HEADS
    x = x_ref[0]
    h = _rms(x, g_ref[0]).astype(BF16)
    q = _dot(h, wq_ref[0]).astype(BF16)
    for hh in range(XA_HEADS):
        cols = slice(hh * hd, (hh + 1) * hd)
        s = _dot(q[:, cols], kt_ref[0, 0, cols, :]) * (hd ** -0.5)
        e = jnp.exp(s - jnp.max(s, axis=-1, keepdims=True))
        p = e / jnp.sum(e, axis=-1, keepdims=True)
        o_ref[:, cols] = _dot(p.astype(BF16), v_ref[0, 0, :, cols]).astype(BF16)
    xn = x + _dot(o_ref[...], wo_ref[0])
    if final:
        xn = _rms(xn, gf_ref[...])
    xo_ref[0] = xn


def _attn_prompt(x, layer, norm_g, w_q, kt, vb, w_o, final_g, final):
    b, seq, d = x.shape
    m = vb.shape[2]
    rows = PROMPT_ROWS
    xspec = pl.BlockSpec((1, rows, d), lambda i, s: (i, s, 0))
    return pl.pallas_call(
        functools.partial(_attn_prompt_kernel, final=final),
        grid=(b, seq // rows),
        in_specs=[xspec, _layer_block(norm_g, layer), _layer_block(w_q, layer),
                  pl.BlockSpec((1, 1, d, m), lambda i, s: (layer, i, 0, 0)),
                  pl.BlockSpec((1, 1, m, d), lambda i, s: (layer, i, 0, 0)),
                  _layer_block(w_o, layer),
                  pl.BlockSpec((1, d), lambda i, s: (0, 0), pipeline_mode=pl.Buffered(1))],
        out_specs=xspec,
        out_shape=jax.ShapeDtypeStruct(x.shape, F32),
        scratch_shapes=[pltpu.VMEM((rows, d), BF16)],
        compiler_params=_params(2),
        name="attn_prompt",
    )(x, norm_g, w_q, kt, vb, w_o, final_g)


def _sample_enter(refs, first):
    if first:
        return refs[0][...], refs[1:]
    xs_ref, o_ref, wo_ref = refs[:3]
    return _sample_residual(xs_ref[...], o_ref, wo_ref), refs[3:]


def _sample_residual(xs, o_ref, wo_ref):
    n, d = xs.shape
    rows = o_ref.shape[0] // n
    hd = d // XA_HEADS
    acc = xs
    for r in range(rows):
        f = _tile_row_feature(r, hd)
        o_r = o_ref[pl.ds(r, n, stride=rows), :].astype(BF16)
        acc = acc + _dot(o_r, wo_ref[0, f:f + LANES, :])
    return acc


def _sample_queries(xs, g_ref, wq_ref, q_ref):
    q = _dot(_rms(xs, g_ref[0]).astype(BF16), wq_ref[0])
    q_ref[...] = _to_tile_rows(q, q_ref.shape[1])


def _prev_attention_specs(o_rows, w_o, layer):
    return [pl.BlockSpec(o_rows.shape, lambda i: (0, 0)), _layer_block(w_o, layer - 1)]


def _even_sample_kernel(*refs, pos0, first):
    xs, refs = _sample_enter(refs, first)
    (st_ref, g_ref, w_in_ref, maps_ref, pscale_ref, w00_ref, b0_ref, sgg_ref, w_out_ref, gx_ref, wq_ref,
     xo_ref, pool_ref, vn_ref, q_ref, y_ref) = refs
    w_a, w_b = pscale_ref.shape[-1], sgg_ref.shape[-1]
    gw = w_a // POOL_GROUPS
    h = _rms(xs, g_ref[0]).astype(BF16)
    xa = _dot(h, w_in_ref[0, :, 0:w_a])
    ga = _dot(h, w_in_ref[0, :, w_a:2 * w_a])
    u = _dot(h, w_in_ref[0, :, 2 * w_a:2 * w_a + w_b])
    v = _dot(h, w_in_ref[0, :, 2 * w_a + w_b:2 * w_a + 2 * w_b])
    gb = _dot(h, w_in_ref[0, :, 2 * w_a + 2 * w_b:])

    for g, w in enumerate(POOL_WINDOWS):
        cols = slice(g * gw, (g + 1) * gw)
        acc = xa[:, cols]
        for k in range(POOL_CTX - (w - 1), POOL_CTX):
            acc = acc + st_ref[0, k, :, cols]
        pooled = acc / float(min(pos0 + 1, w)) - xa[:, cols]
        ya = _dot(pooled.astype(BF16), maps_ref[0, g]) * pscale_ref[0, :, cols] * _silu(ga[:, cols])
        y_ref[:, cols] = ya.astype(BF16)
    pool_ref[0, 0:POOL_CTX - 1] = st_ref[0, 1:POOL_CTX]
    pool_ref[0, POOL_CTX - 1] = xa

    vn = _rms(v, sgg_ref[0])
    vn_ref[0] = vn
    mixed = w00_ref[0] * vn + b0_ref[0]
    y_ref[:, w_a:] = (u * mixed * _silu(gb)).astype(BF16)
    xn = xs + _dot(y_ref[...], w_out_ref[0])
    xo_ref[...] = xn
    _sample_queries(xn, gx_ref, wq_ref, q_ref)


def _even_sample(xs, prev, layer, j, state, norm_g, w_in, maps, pscale, w00, b0, sgg, w_out,
                 norm_xg, w_q, w_o, pos0):
    n, d = xs.shape
    _, ctx, _, w_a = state.shape
    w_b = sgg.shape[-1]
    whole = pl.BlockSpec((n, d), lambda i: (0, 0))
    qshape = (n, d // LANES, LANES)
    first = prev is None
    head_specs = [whole] + ([] if first else _prev_attention_specs(prev, w_o, layer))
    head_args = (xs,) if first else (xs, prev, w_o)
    return pl.pallas_call(
        functools.partial(_even_sample_kernel, pos0=pos0, first=first),
        grid=(1,),
        in_specs=head_specs + [
            _layer_block(state, j), _layer_block(norm_g, layer), _layer_block(w_in, j),
            _layer_block(maps, j), _layer_block(pscale, j), _layer_block(w00, j), _layer_block(b0, j),
            _layer_block(sgg, j), _layer_block(w_out, j), _layer_block(norm_xg, layer),
            _layer_block(w_q, layer)],
        out_specs=[whole, pl.BlockSpec((1, ctx, n, w_a), lambda i: (0, 0, 0, 0)),
                   pl.BlockSpec((1, n, w_b), lambda i: (0, 0, 0)),
                   pl.BlockSpec(qshape, lambda i: (0, 0, 0))],
        out_shape=[jax.ShapeDtypeStruct((n, d), F32),
                   jax.ShapeDtypeStruct((1, ctx, n, w_a), F32),
                   jax.ShapeDtypeStruct((1, n, w_b), F32),
                   jax.ShapeDtypeStruct(qshape, F32)],
        scratch_shapes=[pltpu.VMEM((n, w_a + w_b), BF16)],
        compiler_params=_params(1),
        name="even_sample",
    )(*head_args, state, norm_g, w_in, maps, pscale, w00, b0, sgg, w_out, norm_xg, w_q)


def _odd_sample_kernel(*refs, first):
    xs, refs = _sample_enter(refs, first)
    (st_ref, g_ref, w_in_ref, cw_ref, w_out_ref, gx_ref, wq_ref,
     xo_ref, conv_ref, q_ref, y_ref) = refs
    w_c = cw_ref.shape[-1]
    h = _rms(xs, g_ref[0]).astype(BF16)
    bg = _dot(h, w_in_ref[0, :, 0:w_c])
    cg = _dot(h, w_in_ref[0, :, w_c:2 * w_c])
    xc = _dot(h, w_in_ref[0, :, 2 * w_c:3 * w_c])
    gate = _dot(h, w_in_ref[0, :, 3 * w_c:])
    e = cg * xc
    y = (cw_ref[0, 0:1, :] * st_ref[:, 0:w_c] + cw_ref[0, 1:2, :] * st_ref[:, w_c:]
         + cw_ref[0, 2:3, :] * e)
    conv_ref[:, 0:w_c] = st_ref[:, w_c:]
    conv_ref[:, w_c:] = e
    y_ref[...] = (bg * y * _silu(gate)).astype(BF16)
    xn = xs + _dot(y_ref[...], w_out_ref[0])
    xo_ref[...] = xn
    _sample_queries(xn, gx_ref, wq_ref, q_ref)


def _odd_sample(xs, prev, layer, j, state, norm_g, w_in, conv_w, w_out, norm_xg, w_q, w_o):
    n, d = xs.shape
    w_c = conv_w.shape[-1]
    whole = pl.BlockSpec((n, d), lambda i: (0, 0))
    stspec = pl.BlockSpec((n, CONV_CTX * w_c), lambda i: (0, 0))
    qshape = (n, d // LANES, LANES)
    first = prev is None
    head_specs = [whole] + ([] if first else _prev_attention_specs(prev, w_o, layer))
    head_args = (xs,) if first else (xs, prev, w_o)
    return pl.pallas_call(
        functools.partial(_odd_sample_kernel, first=first),
        grid=(1,),
        in_specs=head_specs + [
            stspec, _layer_block(norm_g, layer), _layer_block(w_in, j), _layer_block(conv_w, j),
            _layer_block(w_out, j), _layer_block(norm_xg, layer), _layer_block(w_q, layer)],
        out_specs=[whole, stspec, pl.BlockSpec(qshape, lambda i: (0, 0, 0))],
        out_shape=[jax.ShapeDtypeStruct((n, d), F32),
                   jax.ShapeDtypeStruct((n, CONV_CTX * w_c), F32),
                   jax.ShapeDtypeStruct(qshape, F32)],
        scratch_shapes=[pltpu.VMEM((n, w_c), BF16)],
        compiler_params=_params(1),
        name="odd_sample",
    )(*head_args, state, norm_g, w_in, conv_w, w_out, norm_xg, w_q)


def _sample_finish_kernel(xs_ref, o_ref, wo_ref, gf_ref, y_ref):
    y_ref[...] = _rms(_sample_residual(xs_ref[...], o_ref, wo_ref), gf_ref[...])


def _sample_finish(xs, prev, w_o, final_g):
    n, d = xs.shape
    whole = pl.BlockSpec((n, d), lambda i: (0, 0))
    return pl.pallas_call(
        _sample_finish_kernel,
        grid=(1,),
        in_specs=[whole] + _prev_attention_specs(prev, w_o, w_o.shape[0])
                 + [pl.BlockSpec((1, d), lambda i: (0, 0))],
        out_specs=whole,
        out_shape=jax.ShapeDtypeStruct((n, d), F32),
        compiler_params=_params(1),
        name="sample_finish",
    )(xs, prev, w_o, final_g)


def kernel(x_prompt, x_sample, mem_prompt, state_pool, state_conv, cache_mem_k, cache_mem_v, norm_mix_g, norm_xattn_g, norm_mem_g, w_in_ab, pool_maps, pool_scale, sgu_w, sgu_b, sgu_g, w_out_ab, w_in_c, conv_w, w_out_c, w_q, w_k, w_v, w_o, norm_final_g):
    depth, d = norm_mix_g.shape
    n_s, dec_seq, _ = x_sample.shape
    n_even, n_odd = pool_scale.shape[0], conv_w.shape[0]
    w_a, w_b, w_c = pool_scale.shape[-1], sgu_g.shape[-1], conv_w.shape[-1]
    hd = d // XA_HEADS
    assert dec_seq == 1, "sample group is one new token per sequence"
    assert PAST_LEN % CHUNK == 0, "the sample token must open a spatial-gating chunk"
    assert (hd // LANES) * XA_HEADS == SUBLANES, "one memory token per (8, 128) register tile"

    bf = lambda a: a.astype(BF16)
    w_in_ab_b, w_out_ab_b, maps_b = bf(w_in_ab), bf(w_out_ab), bf(pool_maps)
    w_in_c_b, w_out_c_b = bf(w_in_c), bf(w_out_c)
    w_q_b, w_k_b, w_v_b, w_o_b = bf(w_q), bf(w_k), bf(w_v), bf(w_o)
    row = lambda a: a.reshape(a.shape[0], 1, a.shape[-1])
    g_mix, g_xattn, g_mem = row(norm_mix_g), row(norm_xattn_g), row(norm_mem_g)
    pscale, sgg = row(pool_scale), row(sgu_g)
    g_final = norm_final_g.reshape(1, d)
    sw = w_b // SGU_GROUPS
    sgbias = jnp.repeat(jnp.swapaxes(sgu_b, 1, 2), sw, axis=2)
    w00 = row(jnp.repeat(sgu_w[:, :, 0, 0], sw, axis=1))
    b0 = row(jnp.repeat(sgu_b[:, :, 0], sw, axis=1))

    k_p, v_p, kt_p, vb_p = _memory_kv(mem_prompt, g_mem, w_k_b, w_v_b)
    cache_k = _lane_tiled(cache_mem_k)
    cache_v = _lane_tiled(cache_mem_v)
    pool_state = jnp.swapaxes(state_pool, 1, 2)
    conv_state = state_conv.reshape(n_odd, n_s, CONV_CTX * w_c)

    xp = x_prompt
    xs = x_sample.reshape(n_s, d)
    so = None
    pool_p, pool_s, conv_p, conv_s, vrows_s = [], [], [], [], []
    for i in range(depth):
        j = i // 2
        if i % 2 == 0:
            xs, pool_new, vn, sq = _even_sample(xs, so, i, j, pool_state, g_mix, w_in_ab_b, maps_b,
                                                pscale, w00, b0, sgg, w_out_ab_b, g_xattn, w_q_b,
                                                w_o_b, PAST_LEN)
            pool_s.append(pool_new)
            vrows_s.append(vn)
            xp, tail, so = _even_prompt(xp, i, j, g_mix, w_in_ab_b, maps_b, pscale, sgu_w, sgbias, sgg,
                                        w_out_ab_b, sq, cache_k, cache_v)
            pool_p.append(tail[:, POOL_HIST - POOL_CTX:])
        else:
            xs, conv_new, sq = _odd_sample(xs, so, i, j, conv_state[j], g_mix, w_in_c_b, conv_w,
                                           w_out_c_b, g_xattn, w_q_b, w_o_b)
            conv_s.append(conv_new.reshape(n_s, CONV_CTX, w_c))
            xp, tail, so = _odd_prompt(xp, i, j, g_mix, w_in_c_b, conv_w, w_out_c_b, sq, cache_k, cache_v)
            conv_p.append(tail[:, CONV_HIST - CONV_CTX:])
        so = so.reshape(n_s * so.shape[1], LANES)
        xp = _attn_prompt(xp, i, g_xattn, w_q_b, kt_p, vb_p, w_o_b, g_final, i == depth - 1)
    ys = _sample_finish(xs, so, w_o_b, g_final)

    new_pool_s = jnp.swapaxes(jnp.concatenate(pool_s, axis=0), 1, 2)
    new_vrows = jnp.concatenate(vrows_s, axis=0).reshape(n_even, n_s, 1, w_b)
    return (xp, ys.reshape(n_s, 1, d), jnp.stack(pool_p), new_pool_s, jnp.stack(conv_p),
            jnp.stack(conv_s), new_vrows, _from_lane_tiled(k_p), _from_lane_tiled(v_p))
```

```python
import functools

import jax
import jax.numpy as jnp
from jax import lax
from jax.experimental import pallas as pl
from jax.experimental.pallas import tpu as pltpu

POOL_WINDOWS = (2, 4, 8, 16)
POOL_GROUPS = len(POOL_WINDOWS)
POOL_CTX = max(POOL_WINDOWS) - 1
SGU_GROUPS = 4
CHUNK = 128
CONV_WIDTH = 3
CONV_CTX = CONV_WIDTH - 1
XA_HEADS = 4
EPS = 1e-6
PAST_LEN = 16384

V7X_VMEM_LIMIT_BYTES = 60 * 1024 * 1024
SUBLANES = 8
LANES = 128
POOL_HIST = 16
CONV_HIST = 8

PROMPT_ROWS = 512
ATTN_ROWS = 1024
MEMKV_BATCH = 2
SAMPLE_TOKEN_BLOCK = 32

F32 = jnp.float32
BF16 = jnp.bfloat16


def _rms(x, g):
    return x * lax.rsqrt(jnp.mean(x * x, axis=-1, keepdims=True) + EPS) * g


def _silu(x):
    return x * jax.nn.sigmoid(x)


def _dot(a, b):
    return jnp.dot(a, b, preferred_element_type=F32)


def _layer_block(arr, layer):
    index = (layer,) + (0,) * (arr.ndim - 1)
    return pl.BlockSpec((1,) + arr.shape[1:], lambda *_: index, pipeline_mode=pl.Buffered(1))


def _params(n_grid):
    return pltpu.CompilerParams(dimension_semantics=("arbitrary",) * n_grid,
                                vmem_limit_bytes=V7X_VMEM_LIMIT_BYTES)


def _tile_row_feature(r, hd):
    return (r % XA_HEADS) * hd + (r // XA_HEADS) * LANES


def _lane_tiled(cache):
    l, b, m, h, hd = cache.shape
    t = hd // LANES
    return cache.reshape(l, b, m, h, t, LANES).transpose(0, 1, 2, 4, 3, 5).reshape(l, b, m, t * h, LANES)


def _from_lane_tiled(tiled):
    l, b, m, r, lanes = tiled.shape
    t = r // XA_HEADS
    return tiled.reshape(l, b, m, t, XA_HEADS, lanes).transpose(0, 1, 2, 4, 3, 5).reshape(
        l, b, m, XA_HEADS, t * lanes)


def _to_tile_rows(a, rows):
    hd = a.shape[1] // XA_HEADS
    pieces = [a[:, _tile_row_feature(r, hd):_tile_row_feature(r, hd) + LANES] for r in range(rows)]
    return jnp.swapaxes(jnp.stack(pieces, axis=0), 0, 1)


def _memkv_kernel(mem_ref, g_ref, wk_ref, wv_ref, k_ref, v_ref, kt_ref, vb_ref):
    nb, m, d = mem_ref.shape
    rows = k_ref.shape[3]
    mn = _rms(mem_ref[...].reshape(nb * m, d), g_ref[0]).astype(BF16)
    k = _dot(mn, wk_ref[0].astype(BF16))
    v = _dot(mn, wv_ref[0].astype(BF16))
    for i in range(nb):
        tok = slice(i * m, (i + 1) * m)
        k_ref[0, i] = _to_tile_rows(k[tok], rows)
        v_ref[0, i] = _to_tile_rows(v[tok], rows)
        kt_ref[0, i] = k[tok].T.astype(BF16)
        vb_ref[0, i] = v[tok].astype(BF16)


def _memory_kv(mem, norm_g, w_k, w_v):
    depth, _, d = norm_g.shape
    b, m, _ = mem.shape
    rows = d // LANES
    nb = MEMKV_BATCH
    wspec = pl.BlockSpec((1, d, d), lambda l, i: (l, 0, 0))
    tiled = pl.BlockSpec((1, nb, m, rows, LANES), lambda l, i: (l, i, 0, 0, 0))
    return pl.pallas_call(
        _memkv_kernel,
        grid=(depth, b // nb),
        in_specs=[pl.BlockSpec((nb, m, d), lambda l, i: (i, 0, 0)),
                  pl.BlockSpec((1, 1, d), lambda l, i: (l, 0, 0)),
                  wspec, wspec],
        out_specs=[tiled, tiled,
                   pl.BlockSpec((1, nb, d, m), lambda l, i: (l, i, 0, 0)),
                   pl.BlockSpec((1, nb, m, d), lambda l, i: (l, i, 0, 0))],
        out_shape=[jax.ShapeDtypeStruct((depth, b, m, rows, LANES), F32),
                   jax.ShapeDtypeStruct((depth, b, m, rows, LANES), F32),
                   jax.ShapeDtypeStruct((depth, b, d, m), BF16),
                   jax.ShapeDtypeStruct((depth, b, m, d), BF16)],
        compiler_params=_params(2),
        name="memory_kv",
    )(mem, norm_g, w_k, w_v)


def _sample_attention(q_ref, k_ref, v_ref, o_ref):
    n, rows, lanes = q_ref.shape
    m = k_ref.shape[2]
    hd = rows * lanes // XA_HEADS
    for i in range(n):
        qt = q_ref[i] * (hd ** -0.5)
        mx = den = acc = None
        for t0 in range(0, m, SAMPLE_TOKEN_BLOCK):
            tok = slice(t0, t0 + SAMPLE_TOKEN_BLOCK)
            part = jnp.sum(k_ref[0, i, tok] * qt[None], axis=-1, keepdims=True)
            s = part + pltpu.roll(part, XA_HEADS, 1)
            block_max = jnp.max(s, axis=0)
            new_mx = block_max if mx is None else jnp.maximum(mx, block_max)
            e = jnp.exp(s - new_mx)
            block_den = jnp.sum(e, axis=0)
            block_acc = jnp.sum(e * v_ref[0, i, tok], axis=0)
            if mx is None:
                den, acc = block_den, block_acc
            else:
                alpha = jnp.exp(mx - new_mx)
                den = den * alpha + block_den
                acc = acc * alpha + block_acc
            mx = new_mx
        o_ref[i] = acc / den


def _sample_attention_specs(q_tiles, cache_k, layer, n_steps):
    n, rows, lanes = q_tiles.shape
    m = cache_k.shape[2]
    per_step = n // (n_steps[0] * n_steps[1])
    step = lambda i, s: i * n_steps[1] + s
    qspec = pl.BlockSpec((per_step, rows, lanes), lambda i, s: (step(i, s), 0, 0))
    kvspec = pl.BlockSpec((1, per_step, m, rows, lanes), lambda i, s: (layer, step(i, s), 0, 0, 0))
    return qspec, kvspec


def _even_prompt_kernel(x_ref, g_ref, w_in_ref, maps_ref, pscale_ref, sgw_ref, sgbias_ref, sgg_ref,
                        w_out_ref, sq_ref, sk_ref, sv_ref, xo_ref, tail_ref, so_ref,
                        z_ref, y_ref, hist_ref):
    rows, w_a = y_ref.shape[0], pscale_ref.shape[-1]
    w_b = sgg_ref.shape[-1]
    gw = w_a // POOL_GROUPS
    sw = w_b // SGU_GROUPS
    s = pl.program_id(1)

    @pl.when(s == 0)
    def _():
        hist_ref[...] = jnp.zeros_like(hist_ref)

    x = x_ref[0]
    h = _rms(x, g_ref[0]).astype(BF16)
    z_ref[...] = _dot(h, w_in_ref[0])

    pos = s * rows + lax.broadcasted_iota(jnp.int32, (rows, gw), 0)
    for g, w in enumerate(POOL_WINDOWS):
        cols = slice(g * gw, (g + 1) * gw)
        xa = z_ref[:, cols]
        acc = jnp.concatenate([hist_ref[:, cols], xa], axis=0)
        shift = 1
        while shift < w:
            acc = acc + pltpu.roll(acc, shift, 0)
            shift *= 2
        cnt = jnp.minimum(pos + 1, w).astype(F32)
        pooled = acc[POOL_HIST:] / cnt - xa
        ya = _dot(pooled.astype(BF16), maps_ref[0, g]) * pscale_ref[0, :, cols]
        ya = ya * _silu(z_ref[:, w_a + g * gw:w_a + (g + 1) * gw])
        y_ref[:, cols] = ya.astype(BF16)
    tail = z_ref[rows - POOL_HIST:, 0:w_a]
    hist_ref[...] = tail
    tail_ref[0] = tail

    vn = _rms(z_ref[:, 2 * w_a + w_b:2 * w_a + 2 * w_b], sgg_ref[0]).astype(BF16)
    causal = (lax.broadcasted_iota(jnp.int32, (CHUNK, CHUNK), 0)
              >= lax.broadcasted_iota(jnp.int32, (CHUNK, CHUNK), 1))
    for g in range(SGU_GROUPS):
        cols = slice(g * sw, (g + 1) * sw)
        wg = jnp.where(causal, sgw_ref[0, g], 0.0).astype(BF16)
        for c in range(rows // CHUNK):
            rs = slice(c * CHUNK, (c + 1) * CHUNK)
            mixed = _dot(wg, vn[rs, cols]) + sgbias_ref[0, :, cols]
            u = z_ref[rs, 2 * w_a + g * sw:2 * w_a + (g + 1) * sw]
            gb = z_ref[rs, 2 * w_a + 2 * w_b + g * sw:2 * w_a + 2 * w_b + (g + 1) * sw]
            y_ref[rs, w_a + g * sw:w_a + (g + 1) * sw] = (u * mixed * _silu(gb)).astype(BF16)

    xo_ref[0] = x + _dot(y_ref[...], w_out_ref[0])
    _sample_attention(sq_ref, sk_ref, sv_ref, so_ref)


def _even_prompt(x, layer, j, norm_g, w_in, maps, pscale, sgw, sgbias, sgg, w_out, sq, cache_k, cache_v):
    b, seq, d = x.shape
    w_a, w_b = pscale.shape[-1], sgg.shape[-1]
    rows = PROMPT_ROWS
    xspec = pl.BlockSpec((1, rows, d), lambda i, s: (i, s, 0))
    qspec, kvspec = _sample_attention_specs(sq, cache_k, layer, (b, seq // rows))
    return pl.pallas_call(
        _even_prompt_kernel,
        grid=(b, seq // rows),
        in_specs=[xspec, _layer_block(norm_g, layer), _layer_block(w_in, j), _layer_block(maps, j),
                  _layer_block(pscale, j), _layer_block(sgw, j), _layer_block(sgbias, j),
                  _layer_block(sgg, j), _layer_block(w_out, j), qspec, kvspec, kvspec],
        out_specs=[xspec, pl.BlockSpec((1, POOL_HIST, w_a), lambda i, s: (i, 0, 0)), qspec],
        out_shape=[jax.ShapeDtypeStruct(x.shape, F32),
                   jax.ShapeDtypeStruct((b, POOL_HIST, w_a), F32),
                   jax.ShapeDtypeStruct(sq.shape, F32)],
        scratch_shapes=[pltpu.VMEM((rows, w_in.shape[-1]), F32),
                        pltpu.VMEM((rows, w_a + w_b), BF16),
                        pltpu.VMEM((POOL_HIST, w_a), F32)],
        compiler_params=_params(2),
        name="even_prompt",
    )(x, norm_g, w_in, maps, pscale, sgw, sgbias, sgg, w_out, sq, cache_k, cache_v)


def _odd_prompt_kernel(x_ref, g_ref, w_in_ref, cw_ref, w_out_ref, sq_ref, sk_ref, sv_ref,
                       xo_ref, tail_ref, so_ref, z_ref, y_ref, hist_ref, *, col_slab):
    rows, w_c = y_ref.shape
    s = pl.program_id(1)

    @pl.when(s == 0)
    def _():
        hist_ref[...] = jnp.zeros_like(hist_ref)

    x = x_ref[0]
    h = _rms(x, g_ref[0]).astype(BF16)
    for n, c0 in enumerate(range(0, w_c, col_slab)):
        cols = slice(c0, c0 + col_slab)
        zb = z_ref.at[n % 2]
        for part in range(4):
            zb[part] = _dot(h, w_in_ref[0, :, part * w_c + c0:part * w_c + c0 + col_slab])
        e = zb[1] * zb[2]
        ext = jnp.concatenate([hist_ref[:, cols], e], axis=0)
        y = (cw_ref[0, 0:1, cols] * pltpu.roll(ext, 2, 0)[CONV_HIST:]
             + cw_ref[0, 1:2, cols] * pltpu.roll(ext, 1, 0)[CONV_HIST:]
             + cw_ref[0, 2:3, cols] * e)
        tail = e[rows - CONV_HIST:]
        hist_ref[:, cols] = tail
        tail_ref[0, :, cols] = tail
        y_ref[:, cols] = (zb[0] * y * _silu(zb[3])).astype(BF16)
    xo_ref[0] = x + _dot(y_ref[...], w_out_ref[0])
    _sample_attention(sq_ref, sk_ref, sv_ref, so_ref)


def _odd_prompt(x, layer, j, norm_g, w_in, conv_w, w_out, sq, cache_k, cache_v):
    b, seq, d = x.shape
    w_c = conv_w.shape[-1]
    rows = PROMPT_ROWS
    col_slab = 512
    xspec = pl.BlockSpec((1, rows, d), lambda i, s: (i, s, 0))
    qspec, kvspec = _sample_attention_specs(sq, cache_k, layer, (b, seq // rows))
    return pl.pallas_call(
        functools.partial(_odd_prompt_kernel, col_slab=col_slab),
        grid=(b, seq // rows),
        in_specs=[xspec, _layer_block(norm_g, layer), _layer_block(w_in, j), _layer_block(conv_w, j),
                  _layer_block(w_out, j), qspec, kvspec, kvspec],
        out_specs=[xspec, pl.BlockSpec((1, CONV_HIST, w_c), lambda i, s: (i, 0, 0)), qspec],
        out_shape=[jax.ShapeDtypeStruct(x.shape, F32),
                   jax.ShapeDtypeStruct((b, CONV_HIST, w_c), F32),
                   jax.ShapeDtypeStruct(sq.shape, F32)],
        scratch_shapes=[pltpu.VMEM((2, 4, rows, col_slab), F32),
                        pltpu.VMEM((rows, w_c), BF16),
                        pltpu.VMEM((CONV_HIST, w_c), F32)],
        compiler_params=_params(2),
        name="odd_prompt",
    )(x, norm_g, w_in, conv_w, w_out, sq, cache_k, cache_v)


def _attn_prompt_kernel(x_ref, g_ref, wq_ref, kt_ref, v_ref, wo_ref, gf_ref, xo_ref, o_ref, *, final):
    d = x_ref.shape[-1]
    hd = d // XA_HEADS
    x = x_ref[0]
    h = _rms(x, g_ref[0]).astype(BF16)
    q = _dot(h, wq_ref[0]).astype(BF16)
    for hh in range(XA_HEADS):
        cols = slice(hh * hd, (hh + 1) * hd)
        s = _dot(q[:, cols], kt_ref[0, 0, cols, :]) * (hd ** -0.5)
        e = jnp.exp(s - jnp.max(s, axis=-1, keepdims=True))
        p = e / jnp.sum(e, axis=-1, keepdims=True)
        o_ref[:, cols] = _dot(p.astype(BF16), v_ref[0, 0, :, cols]).astype(BF16)
    xn = x + _dot(o_ref[...], wo_ref[0])
    if final:
        xn = _rms(xn, gf_ref[...])
    xo_ref[0] = xn


def _attn_prompt(x, layer, norm_g, w_q, kt, vb, w_o, final_g, final):
    b, seq, d = x.shape
    m = vb.shape[2]
    rows = ATTN_ROWS
    xspec = pl.BlockSpec((1, rows, d), lambda i, s: (i, s, 0))
    return pl.pallas_call(
        functools.partial(_attn_prompt_kernel, final=final),
        grid=(b, seq // rows),
        in_specs=[xspec, _layer_block(norm_g, layer), _layer_block(w_q, layer),
                  pl.BlockSpec((1, 1, d, m), lambda i, s: (layer, i, 0, 0)),
                  pl.BlockSpec((1, 1, m, d), lambda i, s: (layer, i, 0, 0)),
                  _layer_block(w_o, layer),
                  pl.BlockSpec((1, d), lambda i, s: (0, 0), pipeline_mode=pl.Buffered(1))],
        out_specs=xspec,
        out_shape=jax.ShapeDtypeStruct(x.shape, F32),
        scratch_shapes=[pltpu.VMEM((rows, d), BF16)],
        compiler_params=_params(2),
        name="attn_prompt",
    )(x, norm_g, w_q, kt, vb, w_o, final_g)


def _sample_enter(refs, first):
    if first:
        return refs[0][...], refs[1:]
    xs_ref, o_ref, wo_ref = refs[:3]
    return _sample_residual(xs_ref[...], o_ref, wo_ref), refs[3:]


def _sample_residual(xs, o_ref, wo_ref):
    n, d = xs.shape
    rows = o_ref.shape[0] // n
    hd = d // XA_HEADS
    acc = xs
    for r in range(rows):
        f = _tile_row_feature(r, hd)
        o_r = o_ref[pl.ds(r, n, stride=rows), :].astype(BF16)
        acc = acc + _dot(o_r, wo_ref[0, f:f + LANES, :])
    return acc


def _sample_queries(xs, g_ref, wq_ref, q_ref):
    q = _dot(_rms(xs, g_ref[0]).astype(BF16), wq_ref[0])
    q_ref[...] = _to_tile_rows(q, q_ref.shape[1])


def _prev_attention_specs(o_rows, w_o, layer):
    return [pl.BlockSpec(o_rows.shape, lambda i: (0, 0)), _layer_block(w_o, layer - 1)]


def _even_sample_kernel(*refs, pos0, first):
    xs, refs = _sample_enter(refs, first)
    (st_ref, g_ref, w_in_ref, maps_ref, pscale_ref, w00_ref, b0_ref, sgg_ref, w_out_ref, gx_ref, wq_ref,
     xo_ref, pool_ref, vn_ref, q_ref, y_ref) = refs
    w_a, w_b = pscale_ref.shape[-1], sgg_ref.shape[-1]
    gw = w_a // POOL_GROUPS
    h = _rms(xs, g_ref[0]).astype(BF16)
    xa = _dot(h, w_in_ref[0, :, 0:w_a])
    ga = _dot(h, w_in_ref[0, :, w_a:2 * w_a])
    u = _dot(h, w_in_ref[0, :, 2 * w_a:2 * w_a + w_b])
    v = _dot(h, w_in_ref[0, :, 2 * w_a + w_b:2 * w_a + 2 * w_b])
    gb = _dot(h, w_in_ref[0, :, 2 * w_a + 2 * w_b:])

    for g, w in enumerate(POOL_WINDOWS):
        cols = slice(g * gw, (g + 1) * gw)
        acc = xa[:, cols]
        for k in range(POOL_CTX - (w - 1), POOL_CTX):
            acc = acc + st_ref[0, k, :, cols]
        pooled = acc / float(min(pos0 + 1, w)) - xa[:, cols]
        ya = _dot(pooled.astype(BF16), maps_ref[0, g]) * pscale_ref[0, :, cols] * _silu(ga[:, cols])
        y_ref[:, cols] = ya.astype(BF16)
    pool_ref[0, 0:POOL_CTX - 1] = st_ref[0, 1:POOL_CTX]
    pool_ref[0, POOL_CTX - 1] = xa

    vn = _rms(v, sgg_ref[0])
    vn_ref[0] = vn
    mixed = w00_ref[0] * vn + b0_ref[0]
    y_ref[:, w_a:] = (u * mixed * _silu(gb)).astype(BF16)
    xn = xs + _dot(y_ref[...], w_out_ref[0])
    xo_ref[...] = xn
    _sample_queries(xn, gx_ref, wq_ref, q_ref)


def _even_sample(xs, prev, layer, j, state, norm_g, w_in, maps, pscale, w00, b0, sgg, w_out,
                 norm_xg, w_q, w_o, pos0):
    n, d = xs.shape
    _, ctx, _, w_a = state.shape
    w_b = sgg.shape[-1]
    whole = pl.BlockSpec((n, d), lambda i: (0, 0))
    qshape = (n, d // LANES, LANES)
    first = prev is None
    head_specs = [whole] + ([] if first else _prev_attention_specs(prev, w_o, layer))
    head_args = (xs,) if first else (xs, prev, w_o)
    return pl.pallas_call(
        functools.partial(_even_sample_kernel, pos0=pos0, first=first),
        grid=(1,),
        in_specs=head_specs + [
            _layer_block(state, j), _layer_block(norm_g, layer), _layer_block(w_in, j),
            _layer_block(maps, j), _layer_block(pscale, j), _layer_block(w00, j), _layer_block(b0, j),
            _layer_block(sgg, j), _layer_block(w_out, j), _layer_block(norm_xg, layer),
            _layer_block(w_q, layer)],
        out_specs=[whole, pl.BlockSpec((1, ctx, n, w_a), lambda i: (0, 0, 0, 0)),
                   pl.BlockSpec((1, n, w_b), lambda i: (0, 0, 0)),
                   pl.BlockSpec(qshape, lambda i: (0, 0, 0))],
        out_shape=[jax.ShapeDtypeStruct((n, d), F32),
                   jax.ShapeDtypeStruct((1, ctx, n, w_a), F32),
                   jax.ShapeDtypeStruct((1, n, w_b), F32),
                   jax.ShapeDtypeStruct(qshape, F32)],
        scratch_shapes=[pltpu.VMEM((n, w_a + w_b), BF16)],
        compiler_params=_params(1),
        name="even_sample",
    )(*head_args, state, norm_g, w_in, maps, pscale, w00, b0, sgg, w_out, norm_xg, w_q)


def _odd_sample_kernel(*refs, first):
    xs, refs = _sample_enter(refs, first)
    (st_ref, g_ref, w_in_ref, cw_ref, w_out_ref, gx_ref, wq_ref,
     xo_ref, conv_ref, q_ref, y_ref) = refs
    w_c = cw_ref.shape[-1]
    h = _rms(xs, g_ref[0]).astype(BF16)
    bg = _dot(h, w_in_ref[0, :, 0:w_c])
    cg = _dot(h, w_in_ref[0, :, w_c:2 * w_c])
    xc = _dot(h, w_in_ref[0, :, 2 * w_c:3 * w_c])
    gate = _dot(h, w_in_ref[0, :, 3 * w_c:])
    e = cg * xc
    y = (cw_ref[0, 0:1, :] * st_ref[:, 0:w_c] + cw_ref[0, 1:2, :] * st_ref[:, w_c:]
         + cw_ref[0, 2:3, :] * e)
    conv_ref[:, 0:w_c] = st_ref[:, w_c:]
    conv_ref[:, w_c:] = e
    y_ref[...] = (bg * y * _silu(gate)).astype(BF16)
    xn = xs + _dot(y_ref[...], w_out_ref[0])
    xo_ref[...] = xn
    _sample_queries(xn, gx_ref, wq_ref, q_ref)


def _odd_sample(xs, prev, layer, j, state, norm_g, w_in, conv_w, w_out, norm_xg, w_q, w_o):
    n, d = xs.shape
    w_c = conv_w.shape[-1]
    whole = pl.BlockSpec((n, d), lambda i: (0, 0))
    stspec = pl.BlockSpec((n, CONV_CTX * w_c), lambda i: (0, 0))
    qshape = (n, d // LANES, LANES)
    first = prev is None
    head_specs = [whole] + ([] if first else _prev_attention_specs(prev, w_o, layer))
    head_args = (xs,) if first else (xs, prev, w_o)
    return pl.pallas_call(
        functools.partial(_odd_sample_kernel, first=first),
        grid=(1,),
        in_specs=head_specs + [
            stspec, _layer_block(norm_g, layer), _layer_block(w_in, j), _layer_block(conv_w, j),
            _layer_block(w_out, j), _layer_block(norm_xg, layer), _layer_block(w_q, layer)],
        out_specs=[whole, stspec, pl.BlockSpec(qshape, lambda i: (0, 0, 0))],
        out_shape=[jax.ShapeDtypeStruct((n, d), F32),
                   jax.ShapeDtypeStruct((n, CONV_CTX * w_c), F32),
                   jax.ShapeDtypeStruct(qshape, F32)],
        scratch_shapes=[pltpu.VMEM((n, w_c), BF16)],
        compiler_params=_params(1),
        name="odd_sample",
    )(*head_args, state, norm_g, w_in, conv_w, w_out, norm_xg, w_q)


def _sample_finish_kernel(xs_ref, o_ref, wo_ref, gf_ref, y_ref):
    y_ref[...] = _rms(_sample_residual(xs_ref[...], o_ref, wo_ref), gf_ref[...])


def _sample_finish(xs, prev, w_o, final_g):
    n, d = xs.shape
    whole = pl.BlockSpec((n, d), lambda i: (0, 0))
    return pl.pallas_call(
        _sample_finish_kernel,
        grid=(1,),
        in_specs=[whole] + _prev_attention_specs(prev, w_o, w_o.shape[0])
                 + [pl.BlockSpec((1, d), lambda i: (0, 0))],
        out_specs=whole,
        out_shape=jax.ShapeDtypeStruct((n, d), F32),
        compiler_params=_params(1),
        name="sample_finish",
    )(xs, prev, w_o, final_g)


def kernel(x_prompt, x_sample, mem_prompt, state_pool, state_conv, cache_mem_k, cache_mem_v, norm_mix_g, norm_xattn_g, norm_mem_g, w_in_ab, pool_maps, pool_scale, sgu_w, sgu_b, sgu_g, w_out_ab, w_in_c, conv_w, w_out_c, w_q, w_k, w_v, w_o, norm_final_g):
    depth, d = norm_mix_g.shape
    n_s, dec_seq, _ = x_sample.shape
    n_even, n_odd = pool_scale.shape[0], conv_w.shape[0]
    w_a, w_b, w_c = pool_scale.shape[-1], sgu_g.shape[-1], conv_w.shape[-1]
    hd = d // XA_HEADS
    assert dec_seq == 1, "sample group is one new token per sequence"
    assert PAST_LEN % CHUNK == 0, "the sample token must open a spatial-gating chunk"
    assert (hd // LANES) * XA_HEADS == SUBLANES, "one memory token per (8, 128) register tile"

    bf = lambda a: a.astype(BF16)
    w_in_ab_b, w_out_ab_b, maps_b = bf(w_in_ab), bf(w_out_ab), bf(pool_maps)
    w_in_c_b, w_out_c_b = bf(w_in_c), bf(w_out_c)
    w_q_b, w_o_b = bf(w_q), bf(w_o)
    row = lambda a: a.reshape(a.shape[0], 1, a.shape[-1])
    g_mix, g_xattn, g_mem = row(norm_mix_g), row(norm_xattn_g), row(norm_mem_g)
    pscale, sgg = row(pool_scale), row(sgu_g)
    g_final = norm_final_g.reshape(1, d)
    sw = w_b // SGU_GROUPS
    sgbias = jnp.repeat(jnp.swapaxes(sgu_b, 1, 2), sw, axis=2)
    w00 = row(jnp.repeat(sgu_w[:, :, 0, 0], sw, axis=1))
    b0 = row(jnp.repeat(sgu_b[:, :, 0], sw, axis=1))

    k_p, v_p, kt_p, vb_p = _memory_kv(mem_prompt, g_mem, w_k, w_v)
    cache_k = _lane_tiled(cache_mem_k)
    cache_v = _lane_tiled(cache_mem_v)
    pool_state = jnp.swapaxes(state_pool, 1, 2)
    conv_state = state_conv.reshape(n_odd, n_s, CONV_CTX * w_c)

    xp = x_prompt
    xs = x_sample.reshape(n_s, d)
    so = None
    pool_p, pool_s, conv_p, conv_s, vrows_s = [], [], [], [], []
    for i in range(depth):
        j = i // 2
        if i % 2 == 0:
            xs, pool_new, vn, sq = _even_sample(xs, so, i, j, pool_state, g_mix, w_in_ab_b, maps_b,
                                                pscale, w00, b0, sgg, w_out_ab_b, g_xattn, w_q_b,
                                                w_o_b, PAST_LEN)
            pool_s.append(pool_new)
            vrows_s.append(vn)
            xp, tail, so = _even_prompt(xp, i, j, g_mix, w_in_ab_b, maps_b, pscale, sgu_w, sgbias, sgg,
                                        w_out_ab_b, sq, cache_k, cache_v)
            pool_p.append(tail[:, POOL_HIST - POOL_CTX:])
        else:
            xs, conv_new, sq = _odd_sample(xs, so, i, j, conv_state[j], g_mix, w_in_c_b, conv_w,
                                           w_out_c_b, g_xattn, w_q_b, w_o_b)
            conv_s.append(conv_new.reshape(n_s, CONV_CTX, w_c))
            xp, tail, so = _odd_prompt(xp, i, j, g_mix, w_in_c_b, conv_w, w_out_c_b, sq, cache_k, cache_v)
            conv_p.append(tail[:, CONV_HIST - CONV_CTX:])
        so = so.reshape(n_s * so.shape[1], LANES)
        xp = _attn_prompt(xp, i, g_xattn, w_q_b, kt_p, vb_p, w_o_b, g_final, i == depth - 1)
    ys = _sample_finish(xs, so, w_o_b, g_final)

    new_pool_s = jnp.swapaxes(jnp.concatenate(pool_s, axis=0), 1, 2)
    new_vrows = jnp.concatenate(vrows_s, axis=0).reshape(n_even, n_s, 1, w_b)
    return (xp, ys.reshape(n_s, 1, d), jnp.stack(pool_p), new_pool_s, jnp.stack(conv_p),
            jnp.stack(conv_s), new_vrows, _from_lane_tiled(k_p), _from_lane_tiled(v_p))
```

```python
import functools

import jax
import jax.numpy as jnp
from jax import lax
from jax.experimental import pallas as pl
from jax.experimental.pallas import tpu as pltpu

POOL_WINDOWS = (2, 4, 8, 16)
POOL_GROUPS = len(POOL_WINDOWS)
POOL_CTX = max(POOL_WINDOWS) - 1
SGU_GROUPS = 4
CHUNK = 128
CONV_WIDTH = 3
CONV_CTX = CONV_WIDTH - 1
XA_HEADS = 4
EPS = 1e-6
PAST_LEN = 16384

V7X_VMEM_LIMIT_BYTES = 60 * 1024 * 1024
SUBLANES = 8
LANES = 128
POOL_HIST = 16
CONV_HIST = 8

PROMPT_ROWS = 512
ATTN_ROWS = 1024
MEMKV_BATCH = 2
SAMPLE_TOKEN_BLOCK = 32

F32 = jnp.float32
BF16 = jnp.bfloat16


def _rms(x, g):
    return x * lax.rsqrt(jnp.mean(x * x, axis=-1, keepdims=True) + EPS) * g


def _silu(x):
    return x * jax.nn.sigmoid(x)


def _dot(a, b):
    return jnp.dot(a, b, preferred_element_type=F32)


def _layer_block(arr, layer):
    index = (layer,) + (0,) * (arr.ndim - 1)
    return pl.BlockSpec((1,) + arr.shape[1:], lambda *_: index, pipeline_mode=pl.Buffered(1))


def _params(n_grid):
    return pltpu.CompilerParams(dimension_semantics=("arbitrary",) * n_grid,
                                vmem_limit_bytes=V7X_VMEM_LIMIT_BYTES)


def _tile_row_feature(r, hd):
    return (r % XA_HEADS) * hd + (r // XA_HEADS) * LANES


def _lane_tiled(cache):
    l, b, m, h, hd = cache.shape
    t = hd // LANES
    return cache.reshape(l, b, m, h, t, LANES).transpose(0, 1, 2, 4, 3, 5).reshape(l, b, m, t * h, LANES)


def _from_lane_tiled(tiled):
    l, b, m, r, lanes = tiled.shape
    t = r // XA_HEADS
    return tiled.reshape(l, b, m, t, XA_HEADS, lanes).transpose(0, 1, 2, 4, 3, 5).reshape(
        l, b, m, XA_HEADS, t * lanes)


def _to_tile_rows(a, rows):
    hd = a.shape[1] // XA_HEADS
    pieces = [a[:, _tile_row_feature(r, hd):_tile_row_feature(r, hd) + LANES] for r in range(rows)]
    return jnp.swapaxes(jnp.stack(pieces, axis=0), 0, 1)


def _memkv_kernel(mem_ref, g_ref, wk_ref, wv_ref, k_ref, v_ref, kt_ref, vb_ref):
    nb, m, d = mem_ref.shape
    rows = k_ref.shape[3]
    mn = _rms(mem_ref[...].reshape(nb * m, d), g_ref[0]).astype(BF16)
    k = _dot(mn, wk_ref[0].astype(BF16))
    v = _dot(mn, wv_ref[0].astype(BF16))
    for i in range(nb):
        tok = slice(i * m, (i + 1) * m)
        k_ref[0, i] = _to_tile_rows(k[tok], rows)
        v_ref[0, i] = _to_tile_rows(v[tok], rows)
        kt_ref[0, i] = k[tok].T.astype(BF16)
        vb_ref[0, i] = v[tok].astype(BF16)


def _memory_kv(mem, norm_g, w_k, w_v):
    depth, _, d = norm_g.shape
    b, m, _ = mem.shape
    rows = d // LANES
    nb = MEMKV_BATCH
    wspec = pl.BlockSpec((1, d, d), lambda l, i: (l, 0, 0))
    tiled = pl.BlockSpec((1, nb, m, rows, LANES), lambda l, i: (l, i, 0, 0, 0))
    return pl.pallas_call(
        _memkv_kernel,
        grid=(depth, b // nb),
        in_specs=[pl.BlockSpec((nb, m, d), lambda l, i: (i, 0, 0)),
                  pl.BlockSpec((1, 1, d), lambda l, i: (l, 0, 0)),
                  wspec, wspec],
        out_specs=[tiled, tiled,
                   pl.BlockSpec((1, nb, d, m), lambda l, i: (l, i, 0, 0)),
                   pl.BlockSpec((1, nb, m, d), lambda l, i: (l, i, 0, 0))],
        out_shape=[jax.ShapeDtypeStruct((depth, b, m, rows, LANES), F32),
                   jax.ShapeDtypeStruct((depth, b, m, rows, LANES), F32),
                   jax.ShapeDtypeStruct((depth, b, d, m), BF16),
                   jax.ShapeDtypeStruct((depth, b, m, d), BF16)],
        compiler_params=_params(2),
        name="memory_kv",
    )(mem, norm_g, w_k, w_v)


def _sample_attention(q_ref, k_ref, v_ref, o_ref):
    n, rows, lanes = q_ref.shape
    m = k_ref.shape[2]
    hd = rows * lanes // XA_HEADS
    for i in range(n):
        qt = q_ref[i] * (hd ** -0.5)
        mx = den = acc = None
        for t0 in range(0, m, SAMPLE_TOKEN_BLOCK):
            tok = slice(t0, t0 + SAMPLE_TOKEN_BLOCK)
            part = jnp.sum(k_ref[0, i, tok] * qt[None], axis=-1, keepdims=True)
            s = part + pltpu.roll(part, XA_HEADS, 1)
            block_max = jnp.max(s, axis=0)
            new_mx = block_max if mx is None else jnp.maximum(mx, block_max)
            e = jnp.exp(s - new_mx)
            block_den = jnp.sum(e, axis=0)
            block_acc = jnp.sum(e * v_ref[0, i, tok], axis=0)
            if mx is None:
                den, acc = block_den, block_acc
            else:
                alpha = jnp.exp(mx - new_mx)
                den = den * alpha + block_den
                acc = acc * alpha + block_acc
            mx = new_mx
        o_ref[i] = acc / den


def _sample_attention_specs(q_tiles, cache_k, layer, n_steps):
    n, rows, lanes = q_tiles.shape
    m = cache_k.shape[2]
    per_step = n // (n_steps[0] * n_steps[1])
    step = lambda i, s: i * n_steps[1] + s
    qspec = pl.BlockSpec((per_step, rows, lanes), lambda i, s: (step(i, s), 0, 0))
    kvspec = pl.BlockSpec((1, per_step, m, rows, lanes), lambda i, s: (layer, step(i, s), 0, 0, 0))
    return qspec, kvspec


def _even_prompt_kernel(x_ref, g_ref, w_in_ref, maps_ref, pscale_ref, sgw_ref, sgbias_ref, sgg_ref,
                        w_out_ref, sq_ref, sk_ref, sv_ref, xo_ref, tail_ref, so_ref,
                        xa_ref, ga_ref, u_ref, v_ref, gb_ref, y_ref, hist_ref):
    rows, w_a = y_ref.shape[0], pscale_ref.shape[-1]
    w_b = sgg_ref.shape[-1]
    gw = w_a // POOL_GROUPS
    sw = w_b // SGU_GROUPS
    s = pl.program_id(1)

    @pl.when(s == 0)
    def _():
        hist_ref[...] = jnp.zeros_like(hist_ref)

    x = x_ref[0]
    h = _rms(x, g_ref[0]).astype(BF16)
    xa_ref[...] = _dot(h, w_in_ref[0, :, 0:w_a])
    v_ref[...] = _dot(h, w_in_ref[0, :, 2 * w_a + w_b:2 * w_a + 2 * w_b])

    pos = s * rows + lax.broadcasted_iota(jnp.int32, (rows, gw), 0)
    pooled = []
    for g, w in enumerate(POOL_WINDOWS):
        cols = slice(g * gw, (g + 1) * gw)
        xa = xa_ref[:, cols]
        acc = jnp.concatenate([hist_ref[:, cols], xa], axis=0)
        shift = 1
        while shift < w:
            acc = acc + pltpu.roll(acc, shift, 0)
            shift *= 2
        cnt = jnp.minimum(pos + 1, w).astype(F32)
        pooled.append((acc[POOL_HIST:] / cnt - xa).astype(BF16))
    tail = xa_ref[rows - POOL_HIST:, :]
    hist_ref[...] = tail
    tail_ref[0] = tail

    ga_ref[...] = _dot(h, w_in_ref[0, :, w_a:2 * w_a])
    for g in range(POOL_GROUPS):
        cols = slice(g * gw, (g + 1) * gw)
        ya = _dot(pooled[g], maps_ref[0, g]) * pscale_ref[0, :, cols] * _silu(ga_ref[:, cols])
        y_ref[:, cols] = ya.astype(BF16)

    u_ref[...] = _dot(h, w_in_ref[0, :, 2 * w_a:2 * w_a + w_b])
    gb_ref[...] = _dot(h, w_in_ref[0, :, 2 * w_a + 2 * w_b:])
    out = x + _dot(y_ref[:, 0:w_a], w_out_ref[0, 0:w_a, :])

    vn = _rms(v_ref[...], sgg_ref[0]).astype(BF16)
    causal = (lax.broadcasted_iota(jnp.int32, (CHUNK, CHUNK), 0)
              >= lax.broadcasted_iota(jnp.int32, (CHUNK, CHUNK), 1))
    for g in range(SGU_GROUPS):
        cols = slice(g * sw, (g + 1) * sw)
        wg = jnp.where(causal, sgw_ref[0, g], 0.0).astype(BF16)
        for c in range(rows // CHUNK):
            rs = slice(c * CHUNK, (c + 1) * CHUNK)
            mixed = _dot(wg, vn[rs, cols]) + sgbias_ref[0, :, cols]
            y_ref[rs, w_a + g * sw:w_a + (g + 1) * sw] = (
                u_ref[rs, cols] * mixed * _silu(gb_ref[rs, cols])).astype(BF16)

    xo_ref[0] = out + _dot(y_ref[:, w_a:], w_out_ref[0, w_a:, :])
    _sample_attention(sq_ref, sk_ref, sv_ref, so_ref)


def _even_prompt(x, layer, j, norm_g, w_in, maps, pscale, sgw, sgbias, sgg, w_out, sq, cache_k, cache_v):
    b, seq, d = x.shape
    w_a, w_b = pscale.shape[-1], sgg.shape[-1]
    rows = PROMPT_ROWS
    xspec = pl.BlockSpec((1, rows, d), lambda i, s: (i, s, 0))
    qspec, kvspec = _sample_attention_specs(sq, cache_k, layer, (b, seq // rows))
    return pl.pallas_call(
        _even_prompt_kernel,
        grid=(b, seq // rows),
        in_specs=[xspec, _layer_block(norm_g, layer), _layer_block(w_in, j), _layer_block(maps, j),
                  _layer_block(pscale, j), _layer_block(sgw, j), _layer_block(sgbias, j),
                  _layer_block(sgg, j), _layer_block(w_out, j), qspec, kvspec, kvspec],
        out_specs=[xspec, pl.BlockSpec((1, POOL_HIST, w_a), lambda i, s: (i, 0, 0)), qspec],
        out_shape=[jax.ShapeDtypeStruct(x.shape, F32),
                   jax.ShapeDtypeStruct((b, POOL_HIST, w_a), F32),
                   jax.ShapeDtypeStruct(sq.shape, F32)],
        scratch_shapes=[pltpu.VMEM((rows, w_a), F32), pltpu.VMEM((rows, w_a), F32),
                        pltpu.VMEM((rows, w_b), F32), pltpu.VMEM((rows, w_b), F32),
                        pltpu.VMEM((rows, w_b), F32),
                        pltpu.VMEM((rows, w_a + w_b), BF16),
                        pltpu.VMEM((POOL_HIST, w_a), F32)],
        compiler_params=_params(2),
        name="even_prompt",
    )(x, norm_g, w_in, maps, pscale, sgw, sgbias, sgg, w_out, sq, cache_k, cache_v)


def _odd_prompt_kernel(x_ref, g_ref, w_in_ref, cw_ref, w_out_ref, sq_ref, sk_ref, sv_ref,
                       xo_ref, tail_ref, so_ref, y_ref, hist_ref, *, col_slab):
    rows, w_c = y_ref.shape
    s = pl.program_id(1)

    @pl.when(s == 0)
    def _():
        hist_ref[...] = jnp.zeros_like(hist_ref)

    x = x_ref[0]
    h = _rms(x, g_ref[0]).astype(BF16)
    for c0 in range(0, w_c, col_slab):
        cols = slice(c0, c0 + col_slab)
        zb = [_dot(h, w_in_ref[0, :, part * w_c + c0:part * w_c + c0 + col_slab]) for part in range(4)]
        e = zb[1] * zb[2]
        ext = jnp.concatenate([hist_ref[:, cols], e], axis=0)
        y = (cw_ref[0, 0:1, cols] * pltpu.roll(ext, 2, 0)[CONV_HIST:]
             + cw_ref[0, 1:2, cols] * pltpu.roll(ext, 1, 0)[CONV_HIST:]
             + cw_ref[0, 2:3, cols] * e)
        tail = e[rows - CONV_HIST:]
        hist_ref[:, cols] = tail
        tail_ref[0, :, cols] = tail
        y_ref[:, cols] = (zb[0] * y * _silu(zb[3])).astype(BF16)
    xo_ref[0] = x + _dot(y_ref[...], w_out_ref[0])
    _sample_attention(sq_ref, sk_ref, sv_ref, so_ref)


def _odd_prompt(x, layer, j, norm_g, w_in, conv_w, w_out, sq, cache_k, cache_v):
    b, seq, d = x.shape
    w_c = conv_w.shape[-1]
    rows = PROMPT_ROWS
    col_slab = 512
    xspec = pl.BlockSpec((1, rows, d), lambda i, s: (i, s, 0))
    qspec, kvspec = _sample_attention_specs(sq, cache_k, layer, (b, seq // rows))
    return pl.pallas_call(
        functools.partial(_odd_prompt_kernel, col_slab=col_slab),
        grid=(b, seq // rows),
        in_specs=[xspec, _layer_block(norm_g, layer), _layer_block(w_in, j), _layer_block(conv_w, j),
                  _layer_block(w_out, j), qspec, kvspec, kvspec],
        out_specs=[xspec, pl.BlockSpec((1, CONV_HIST, w_c), lambda i, s: (i, 0, 0)), qspec],
        out_shape=[jax.ShapeDtypeStruct(x.shape, F32),
                   jax.ShapeDtypeStruct((b, CONV_HIST, w_c), F32),
                   jax.ShapeDtypeStruct(sq.shape, F32)],
        scratch_shapes=[pltpu.VMEM((rows, w_c), BF16),
                        pltpu.VMEM((CONV_HIST, w_c), F32)],
        compiler_params=_params(2),
        name="odd_prompt",
    )(x, norm_g, w_in, conv_w, w_out, sq, cache_k, cache_v)


def _attn_prompt_kernel(x_ref, g_ref, wq_ref, kt_ref, v_ref, wo_ref, gf_ref, xo_ref, o_ref, *, final):
    d = x_ref.shape[-1]
    hd = d // XA_HEADS
    x = x_ref[0]
    h = _rms(x, g_ref[0]).astype(BF16)
    q = _dot(h, wq_ref[0]).astype(BF16)
    heads = [slice(hh * hd, (hh + 1) * hd) for hh in range(XA_HEADS)]
    scores = [_dot(q[:, cols], kt_ref[0, 0, cols, :]) * (hd ** -0.5) for cols in heads]
    for cols, s in zip(heads, scores):
        e = jnp.exp(s - jnp.max(s, axis=-1, keepdims=True))
        p = e / jnp.sum(e, axis=-1, keepdims=True)
        o_ref[:, cols] = _dot(p.astype(BF16), v_ref[0, 0, :, cols]).astype(BF16)
    xn = x + _dot(o_ref[...], wo_ref[0])
    if final:
        xn = _rms(xn, gf_ref[...])
    xo_ref[0] = xn


def _attn_prompt(x, layer, norm_g, w_q, kt, vb, w_o, final_g, final):
    b, seq, d = x.shape
    m = vb.shape[2]
    rows = ATTN_ROWS
    xspec = pl.BlockSpec((1, rows, d), lambda i, s: (i, s, 0))
    return pl.pallas_call(
        functools.partial(_attn_prompt_kernel, final=final),
        grid=(b, seq // rows),
        in_specs=[xspec, _layer_block(norm_g, layer), _layer_block(w_q, layer),
                  pl.BlockSpec((1, 1, d, m), lambda i, s: (layer, i, 0, 0)),
                  pl.BlockSpec((1, 1, m, d), lambda i, s: (layer, i, 0, 0)),
                  _layer_block(w_o, layer),
                  pl.BlockSpec((1, d), lambda i, s: (0, 0), pipeline_mode=pl.Buffered(1))],
        out_specs=xspec,
        out_shape=jax.ShapeDtypeStruct(x.shape, F32),
        scratch_shapes=[pltpu.VMEM((rows, d), BF16)],
        compiler_params=_params(2),
        name="attn_prompt",
    )(x, norm_g, w_q, kt, vb, w_o, final_g)


def _sample_enter(refs, first):
    if first:
        return refs[0][...], refs[1:]
    xs_ref, o_ref, wo_ref = refs[:3]
    return _sample_residual(xs_ref[...], o_ref, wo_ref), refs[3:]


def _sample_residual(xs, o_ref, wo_ref):
    n, d = xs.shape
    rows = o_ref.shape[0] // n
    hd = d // XA_HEADS
    acc = xs
    for r in range(rows):
        f = _tile_row_feature(r, hd)
        o_r = o_ref[pl.ds(r, n, stride=rows), :].astype(BF16)
        acc = acc + _dot(o_r, wo_ref[0, f:f + LANES, :])
    return acc


def _sample_queries(xs, g_ref, wq_ref, q_ref):
    q = _dot(_rms(xs, g_ref[0]).astype(BF16), wq_ref[0])
    q_ref[...] = _to_tile_rows(q, q_ref.shape[1])


def _prev_attention_specs(o_rows, w_o, layer):
    return [pl.BlockSpec(o_rows.shape, lambda i: (0, 0)), _layer_block(w_o, layer - 1)]


def _even_sample_kernel(*refs, pos0, first):
    xs, refs = _sample_enter(refs, first)
    (st_ref, g_ref, w_in_ref, maps_ref, pscale_ref, w00_ref, b0_ref, sgg_ref, w_out_ref, gx_ref, wq_ref,
     xo_ref, pool_ref, vn_ref, q_ref, y_ref) = refs
    w_a, w_b = pscale_ref.shape[-1], sgg_ref.shape[-1]
    gw = w_a // POOL_GROUPS
    h = _rms(xs, g_ref[0]).astype(BF16)
    xa = _dot(h, w_in_ref[0, :, 0:w_a])
    ga = _dot(h, w_in_ref[0, :, w_a:2 * w_a])
    u = _dot(h, w_in_ref[0, :, 2 * w_a:2 * w_a + w_b])
    v = _dot(h, w_in_ref[0, :, 2 * w_a + w_b:2 * w_a + 2 * w_b])
    gb = _dot(h, w_in_ref[0, :, 2 * w_a + 2 * w_b:])

    for g, w in enumerate(POOL_WINDOWS):
        cols = slice(g * gw, (g + 1) * gw)
        acc = xa[:, cols]
        for k in range(POOL_CTX - (w - 1), POOL_CTX):
            acc = acc + st_ref[0, k, :, cols]
        pooled = acc / float(min(pos0 + 1, w)) - xa[:, cols]
        ya = _dot(pooled.astype(BF16), maps_ref[0, g]) * pscale_ref[0, :, cols] * _silu(ga[:, cols])
        y_ref[:, cols] = ya.astype(BF16)
    pool_ref[0, 0:POOL_CTX - 1] = st_ref[0, 1:POOL_CTX]
    pool_ref[0, POOL_CTX - 1] = xa

    vn = _rms(v, sgg_ref[0])
    vn_ref[0] = vn
    mixed = w00_ref[0] * vn + b0_ref[0]
    y_ref[:, w_a:] = (u * mixed * _silu(gb)).astype(BF16)
    xn = xs + _dot(y_ref[...], w_out_ref[0])
    xo_ref[...] = xn
    _sample_queries(xn, gx_ref, wq_ref, q_ref)


def _even_sample(xs, prev, layer, j, state, norm_g, w_in, maps, pscale, w00, b0, sgg, w_out,
                 norm_xg, w_q, w_o, pos0):
    n, d = xs.shape
    _, ctx, _, w_a = state.shape
    w_b = sgg.shape[-1]
    whole = pl.BlockSpec((n, d), lambda i: (0, 0))
    qshape = (n, d // LANES, LANES)
    first = prev is None
    head_specs = [whole] + ([] if first else _prev_attention_specs(prev, w_o, layer))
    head_args = (xs,) if first else (xs, prev, w_o)
    return pl.pallas_call(
        functools.partial(_even_sample_kernel, pos0=pos0, first=first),
        grid=(1,),
        in_specs=head_specs + [
            _layer_block(state, j), _layer_block(norm_g, layer), _layer_block(w_in, j),
            _layer_block(maps, j), _layer_block(pscale, j), _layer_block(w00, j), _layer_block(b0, j),
            _layer_block(sgg, j), _layer_block(w_out, j), _layer_block(norm_xg, layer),
            _layer_block(w_q, layer)],
        out_specs=[whole, pl.BlockSpec((1, ctx, n, w_a), lambda i: (0, 0, 0, 0)),
                   pl.BlockSpec((1, n, w_b), lambda i: (0, 0, 0)),
                   pl.BlockSpec(qshape, lambda i: (0, 0, 0))],
        out_shape=[jax.ShapeDtypeStruct((n, d), F32),
                   jax.ShapeDtypeStruct((1, ctx, n, w_a), F32),
                   jax.ShapeDtypeStruct((1, n, w_b), F32),
                   jax.ShapeDtypeStruct(qshape, F32)],
        scratch_shapes=[pltpu.VMEM((n, w_a + w_b), BF16)],
        compiler_params=_params(1),
        name="even_sample",
    )(*head_args, state, norm_g, w_in, maps, pscale, w00, b0, sgg, w_out, norm_xg, w_q)


def _odd_sample_kernel(*refs, first):
    xs, refs = _sample_enter(refs, first)
    (st_ref, g_ref, w_in_ref, cw_ref, w_out_ref, gx_ref, wq_ref,
     xo_ref, conv_ref, q_ref, y_ref) = refs
    w_c = cw_ref.shape[-1]
    h = _rms(xs, g_ref[0]).astype(BF16)
    bg = _dot(h, w_in_ref[0, :, 0:w_c])
    cg = _dot(h, w_in_ref[0, :, w_c:2 * w_c])
    xc = _dot(h, w_in_ref[0, :, 2 * w_c:3 * w_c])
    gate = _dot(h, w_in_ref[0, :, 3 * w_c:])
    e = cg * xc
    y = (cw_ref[0, 0:1, :] * st_ref[:, 0:w_c] + cw_ref[0, 1:2, :] * st_ref[:, w_c:]
         + cw_ref[0, 2:3, :] * e)
    conv_ref[:, 0:w_c] = st_ref[:, w_c:]
    conv_ref[:, w_c:] = e
    y_ref[...] = (bg * y * _silu(gate)).astype(BF16)
    xn = xs + _dot(y_ref[...], w_out_ref[0])
    xo_ref[...] = xn
    _sample_queries(xn, gx_ref, wq_ref, q_ref)


def _odd_sample(xs, prev, layer, j, state, norm_g, w_in, conv_w, w_out, norm_xg, w_q, w_o):
    n, d = xs.shape
    w_c = conv_w.shape[-1]
    whole = pl.BlockSpec((n, d), lambda i: (0, 0))
    stspec = pl.BlockSpec((n, CONV_CTX * w_c), lambda i: (0, 0))
    qshape = (n, d // LANES, LANES)
    first = prev is None
    head_specs = [whole] + ([] if first else _prev_attention_specs(prev, w_o, layer))
    head_args = (xs,) if first else (xs, prev, w_o)
    return pl.pallas_call(
        functools.partial(_odd_sample_kernel, first=first),
        grid=(1,),
        in_specs=head_specs + [
            stspec, _layer_block(norm_g, layer), _layer_block(w_in, j), _layer_block(conv_w, j),
            _layer_block(w_out, j), _layer_block(norm_xg, layer), _layer_block(w_q, layer)],
        out_specs=[whole, stspec, pl.BlockSpec(qshape, lambda i: (0, 0, 0))],
        out_shape=[jax.ShapeDtypeStruct((n, d), F32),
                   jax.ShapeDtypeStruct((n, CONV_CTX * w_c), F32),
                   jax.ShapeDtypeStruct(qshape, F32)],
        scratch_shapes=[pltpu.VMEM((n, w_c), BF16)],
        compiler_params=_params(1),
        name="odd_sample",
    )(*head_args, state, norm_g, w_in, conv_w, w_out, norm_xg, w_q)


def _sample_finish_kernel(xs_ref, o_ref, wo_ref, gf_ref, y_ref):
    y_ref[...] = _rms(_sample_residual(xs_ref[...], o_ref, wo_ref), gf_ref[...])


def _sample_finish(xs, prev, w_o, final_g):
    n, d = xs.shape
    whole = pl.BlockSpec((n, d), lambda i: (0, 0))
    return pl.pallas_call(
        _sample_finish_kernel,
        grid=(1,),
        in_specs=[whole] + _prev_attention_specs(prev, w_o, w_o.shape[0])
                 + [pl.BlockSpec((1, d), lambda i: (0, 0))],
        out_specs=whole,
        out_shape=jax.ShapeDtypeStruct((n, d), F32),
        compiler_params=_params(1),
        name="sample_finish",
    )(xs, prev, w_o, final_g)


def kernel(x_prompt, x_sample, mem_prompt, state_pool, state_conv, cache_mem_k, cache_mem_v, norm_mix_g, norm_xattn_g, norm_mem_g, w_in_ab, pool_maps, pool_scale, sgu_w, sgu_b, sgu_g, w_out_ab, w_in_c, conv_w, w_out_c, w_q, w_k, w_v, w_o, norm_final_g):
    depth, d = norm_mix_g.shape
    n_s, dec_seq, _ = x_sample.shape
    n_even, n_odd = pool_scale.shape[0], conv_w.shape[0]
    w_a, w_b, w_c = pool_scale.shape[-1], sgu_g.shape[-1], conv_w.shape[-1]
    hd = d // XA_HEADS
    assert dec_seq == 1, "sample group is one new token per sequence"
    assert PAST_LEN % CHUNK == 0, "the sample token must open a spatial-gating chunk"
    assert (hd // LANES) * XA_HEADS == SUBLANES, "one memory token per (8, 128) register tile"

    bf = lambda a: a.astype(BF16)
    w_in_ab_b, w_out_ab_b, maps_b = bf(w_in_ab), bf(w_out_ab), bf(pool_maps)
    w_in_c_b, w_out_c_b = bf(w_in_c), bf(w_out_c)
    w_q_b, w_o_b = bf(w_q), bf(w_o)
    row = lambda a: a.reshape(a.shape[0], 1, a.shape[-1])
    g_mix, g_xattn, g_mem = row(norm_mix_g), row(norm_xattn_g), row(norm_mem_g)
    pscale, sgg = row(pool_scale), row(sgu_g)
    g_final = norm_final_g.reshape(1, d)
    sw = w_b // SGU_GROUPS
    sgbias = jnp.repeat(jnp.swapaxes(sgu_b, 1, 2), sw, axis=2)
    w00 = row(jnp.repeat(sgu_w[:, :, 0, 0], sw, axis=1))
    b0 = row(jnp.repeat(sgu_b[:, :, 0], sw, axis=1))

    k_p, v_p, kt_p, vb_p = _memory_kv(mem_prompt, g_mem, w_k, w_v)
    cache_k = _lane_tiled(cache_mem_k)
    cache_v = _lane_tiled(cache_mem_v)
    pool_state = jnp.swapaxes(state_pool, 1, 2)
    conv_state = state_conv.reshape(n_odd, n_s, CONV_CTX * w_c)

    xp = x_prompt
    xs = x_sample.reshape(n_s, d)
    so = None
    pool_p, pool_s, conv_p, conv_s, vrows_s = [], [], [], [], []
    for i in range(depth):
        j = i // 2
        if i % 2 == 0:
            xs, pool_new, vn, sq = _even_sample(xs, so, i, j, pool_state, g_mix, w_in_ab_b, maps_b,
                                                pscale, w00, b0, sgg, w_out_ab_b, g_xattn, w_q_b,
                                                w_o_b, PAST_LEN)
            pool_s.append(pool_new)
            vrows_s.append(vn)
            xp, tail, so = _even_prompt(xp, i, j, g_mix, w_in_ab_b, maps_b, pscale, sgu_w, sgbias, sgg,
                                        w_out_ab_b, sq, cache_k, cache_v)
            pool_p.append(tail[:, POOL_HIST - POOL_CTX:])
        else:
            xs, conv_new, sq = _odd_sample(xs, so, i, j, conv_state[j], g_mix, w_in_c_b, conv_w,
                                           w_out_c_b, g_xattn, w_q_b, w_o_b)
            conv_s.append(conv_new.reshape(n_s, CONV_CTX, w_c))
            xp, tail, so = _odd_prompt(xp, i, j, g_mix, w_in_c_b, conv_w, w_out_c_b, sq, cache_k, cache_v)
            conv_p.append(tail[:, CONV_HIST - CONV_CTX:])
        so = so.reshape(n_s * so.shape[1], LANES)
        xp = _attn_prompt(xp, i, g_xattn, w_q_b, kt_p, vb_p, w_o_b, g_final, i == depth - 1)
    ys = _sample_finish(xs, so, w_o_b, g_final)

    new_pool_s = jnp.swapaxes(jnp.concatenate(pool_s, axis=0), 1, 2)
    new_vrows = jnp.concatenate(vrows_s, axis=0).reshape(n_even, n_s, 1, w_b)
    return (xp, ys.reshape(n_s, 1, d), jnp.stack(pool_p), new_pool_s, jnp.stack(conv_p),
            jnp.stack(conv_s), new_vrows, _from_lane_tiled(k_p), _from_lane_tiled(v_p))
```

```python
import functools

import jax
import jax.numpy as jnp
from jax import lax
from jax.experimental import pallas as pl
from jax.experimental.pallas import tpu as pltpu

POOL_WINDOWS = (2, 4, 8, 16)
POOL_GROUPS = len(POOL_WINDOWS)
POOL_CTX = max(POOL_WINDOWS) - 1
SGU_GROUPS = 4
CHUNK = 128
CONV_WIDTH = 3
CONV_CTX = CONV_WIDTH - 1
XA_HEADS = 4
EPS = 1e-6
PAST_LEN = 16384

V7X_VMEM_LIMIT_BYTES = 60 * 1024 * 1024
SUBLANES = 8
LANES = 128
POOL_HIST = 16
CONV_HIST = 8

PROMPT_ROWS = 512
ATTN_ROWS = 1024
MEMKV_BATCH = 2
SAMPLE_TOKEN_BLOCK = 32

F32 = jnp.float32
BF16 = jnp.bfloat16


def _rms(x, g):
    return x * lax.rsqrt(jnp.mean(x * x, axis=-1, keepdims=True) + EPS) * g


def _silu(x):
    return x * jax.nn.sigmoid(x)


def _dot(a, b):
    return jnp.dot(a, b, preferred_element_type=F32)


def _layer_block(arr, layer):
    index = (layer,) + (0,) * (arr.ndim - 1)
    return pl.BlockSpec((1,) + arr.shape[1:], lambda *_: index, pipeline_mode=pl.Buffered(1))


def _params(n_grid):
    return pltpu.CompilerParams(dimension_semantics=("arbitrary",) * n_grid,
                                vmem_limit_bytes=V7X_VMEM_LIMIT_BYTES)


def _tile_row_feature(r, hd):
    return (r % XA_HEADS) * hd + (r // XA_HEADS) * LANES


def _lane_tiled(cache):
    l, b, m, h, hd = cache.shape
    t = hd // LANES
    return cache.reshape(l, b, m, h, t, LANES).transpose(0, 1, 2, 4, 3, 5).reshape(l, b, m, t * h, LANES)


def _from_lane_tiled(tiled):
    l, b, m, r, lanes = tiled.shape
    t = r // XA_HEADS
    return tiled.reshape(l, b, m, t, XA_HEADS, lanes).transpose(0, 1, 2, 4, 3, 5).reshape(
        l, b, m, XA_HEADS, t * lanes)


def _to_tile_rows(a, rows):
    hd = a.shape[1] // XA_HEADS
    pieces = [a[:, _tile_row_feature(r, hd):_tile_row_feature(r, hd) + LANES] for r in range(rows)]
    return jnp.swapaxes(jnp.stack(pieces, axis=0), 0, 1)


def _memkv_kernel(mem_ref, g_ref, wk_ref, wv_ref, k_ref, v_ref, kt_ref, vb_ref):
    nb, m, d = mem_ref.shape
    rows = k_ref.shape[3]
    mn = _rms(mem_ref[...].reshape(nb * m, d), g_ref[0]).astype(BF16)
    k = _dot(mn, wk_ref[0].astype(BF16))
    v = _dot(mn, wv_ref[0].astype(BF16))
    for i in range(nb):
        tok = slice(i * m, (i + 1) * m)
        k_ref[0, i] = _to_tile_rows(k[tok], rows)
        v_ref[0, i] = _to_tile_rows(v[tok], rows)
        kt_ref[0, i] = k[tok].T.astype(BF16)
        vb_ref[0, i] = v[tok].astype(BF16)


def _memory_kv(mem, norm_g, w_k, w_v):
    depth, _, d = norm_g.shape
    b, m, _ = mem.shape
    rows = d // LANES
    nb = MEMKV_BATCH
    wspec = pl.BlockSpec((1, d, d), lambda l, i: (l, 0, 0))
    tiled = pl.BlockSpec((1, nb, m, rows, LANES), lambda l, i: (l, i, 0, 0, 0))
    return pl.pallas_call(
        _memkv_kernel,
        grid=(depth, b // nb),
        in_specs=[pl.BlockSpec((nb, m, d), lambda l, i: (i, 0, 0)),
                  pl.BlockSpec((1, 1, d), lambda l, i: (l, 0, 0)),
                  wspec, wspec],
        out_specs=[tiled, tiled,
                   pl.BlockSpec((1, nb, d, m), lambda l, i: (l, i, 0, 0)),
                   pl.BlockSpec((1, nb, m, d), lambda l, i: (l, i, 0, 0))],
        out_shape=[jax.ShapeDtypeStruct((depth, b, m, rows, LANES), F32),
                   jax.ShapeDtypeStruct((depth, b, m, rows, LANES), F32),
                   jax.ShapeDtypeStruct((depth, b, d, m), BF16),
                   jax.ShapeDtypeStruct((depth, b, m, d), BF16)],
        compiler_params=_params(2),
        name="memory_kv",
    )(mem, norm_g, w_k, w_v)


def _sample_attention(q_ref, k_ref, v_ref, o_ref):
    n, rows, lanes = q_ref.shape
    m = k_ref.shape[2]
    hd = rows * lanes // XA_HEADS
    for i in range(n):
        qt = q_ref[i] * (hd ** -0.5)
        mx = den = acc = None
        for t0 in range(0, m, SAMPLE_TOKEN_BLOCK):
            tok = slice(t0, t0 + SAMPLE_TOKEN_BLOCK)
            part = jnp.sum(k_ref[0, i, tok] * qt[None], axis=-1, keepdims=True)
            s = part + pltpu.roll(part, XA_HEADS, 1)
            block_max = jnp.max(s, axis=0)
            new_mx = block_max if mx is None else jnp.maximum(mx, block_max)
            e = jnp.exp(s - new_mx)
            block_den = jnp.sum(e, axis=0)
            block_acc = jnp.sum(e * v_ref[0, i, tok], axis=0)
            if mx is None:
                den, acc = block_den, block_acc
            else:
                alpha = jnp.exp(mx - new_mx)
                den = den * alpha + block_den
                acc = acc * alpha + block_acc
            mx = new_mx
        o_ref[i] = acc / den


def _sample_attention_specs(q_tiles, cache_k, layer, n_steps):
    n, rows, lanes = q_tiles.shape
    m = cache_k.shape[2]
    per_step = n // (n_steps[0] * n_steps[1])
    step = lambda i, s: i * n_steps[1] + s
    qspec = pl.BlockSpec((per_step, rows, lanes), lambda i, s: (step(i, s), 0, 0))
    kvspec = pl.BlockSpec((1, per_step, m, rows, lanes), lambda i, s: (layer, step(i, s), 0, 0, 0))
    return qspec, kvspec


def _even_prompt_kernel(x_ref, g_ref, w_in_ref, maps_ref, pscale_ref, sgw_ref, sgbias_ref, sgg_ref,
                        w_out_ref, sq_ref, sk_ref, sv_ref, xo_ref, tail_ref, so_ref,
                        xa_ref, ga_ref, u_ref, v_ref, gb_ref, y_ref, hist_ref):
    rows, w_a = y_ref.shape[0], pscale_ref.shape[-1]
    w_b = sgg_ref.shape[-1]
    gw = w_a // POOL_GROUPS
    sw = w_b // SGU_GROUPS
    s = pl.program_id(1)

    @pl.when(s == 0)
    def _():
        hist_ref[...] = jnp.zeros_like(hist_ref)

    x = x_ref[0]
    h = _rms(x, g_ref[0]).astype(BF16)
    xa_ref[...] = _dot(h, w_in_ref[0, :, 0:w_a])
    v_ref[...] = _dot(h, w_in_ref[0, :, 2 * w_a + w_b:2 * w_a + 2 * w_b])

    pos = s * rows + lax.broadcasted_iota(jnp.int32, (rows, gw), 0)
    pooled = []
    for g, w in enumerate(POOL_WINDOWS):
        cols = slice(g * gw, (g + 1) * gw)
        xa = xa_ref[:, cols]
        acc = jnp.concatenate([hist_ref[:, cols], xa], axis=0)
        shift = 1
        while shift < w:
            acc = acc + pltpu.roll(acc, shift, 0)
            shift *= 2
        cnt = jnp.minimum(pos + 1, w).astype(F32)
        pooled.append((acc[POOL_HIST:] / cnt - xa).astype(BF16))
    tail = xa_ref[rows - POOL_HIST:, :]
    hist_ref[...] = tail
    tail_ref[0] = tail

    ga_ref[...] = _dot(h, w_in_ref[0, :, w_a:2 * w_a])
    for g in range(POOL_GROUPS):
        cols = slice(g * gw, (g + 1) * gw)
        ya = _dot(pooled[g], maps_ref[0, g]) * pscale_ref[0, :, cols] * _silu(ga_ref[:, cols])
        y_ref[:, cols] = ya.astype(BF16)

    u_ref[...] = _dot(h, w_in_ref[0, :, 2 * w_a:2 * w_a + w_b])
    gb_ref[...] = _dot(h, w_in_ref[0, :, 2 * w_a + 2 * w_b:])
    out = x + _dot(y_ref[:, 0:w_a], w_out_ref[0, 0:w_a, :])

    vn = _rms(v_ref[...], sgg_ref[0]).astype(BF16)
    causal = (lax.broadcasted_iota(jnp.int32, (CHUNK, CHUNK), 0)
              >= lax.broadcasted_iota(jnp.int32, (CHUNK, CHUNK), 1))
    for g in range(SGU_GROUPS):
        cols = slice(g * sw, (g + 1) * sw)
        wg = jnp.where(causal, sgw_ref[0, g], 0.0).astype(BF16)
        for c in range(rows // CHUNK):
            rs = slice(c * CHUNK, (c + 1) * CHUNK)
            mixed = _dot(wg, vn[rs, cols]) + sgbias_ref[0, :, cols]
            y_ref[rs, w_a + g * sw:w_a + (g + 1) * sw] = (
                u_ref[rs, cols] * mixed * _silu(gb_ref[rs, cols])).astype(BF16)

    xo_ref[0] = out + _dot(y_ref[:, w_a:], w_out_ref[0, w_a:, :])
    _sample_attention(sq_ref, sk_ref, sv_ref, so_ref)


def _even_prompt(x, layer, j, norm_g, w_in, maps, pscale, sgw, sgbias, sgg, w_out, sq, cache_k, cache_v):
    b, seq, d = x.shape
    w_a, w_b = pscale.shape[-1], sgg.shape[-1]
    rows = PROMPT_ROWS
    xspec = pl.BlockSpec((1, rows, d), lambda i, s: (i, s, 0))
    qspec, kvspec = _sample_attention_specs(sq, cache_k, layer, (b, seq // rows))
    return pl.pallas_call(
        _even_prompt_kernel,
        grid=(b, seq // rows),
        in_specs=[xspec, _layer_block(norm_g, layer), _layer_block(w_in, 0), _layer_block(maps, 0),
                  _layer_block(pscale, j), _layer_block(sgw, j), _layer_block(sgbias, j),
                  _layer_block(sgg, j), _layer_block(w_out, 0), qspec, kvspec, kvspec],
        out_specs=[xspec, pl.BlockSpec((1, POOL_HIST, w_a), lambda i, s: (i, 0, 0)), qspec],
        out_shape=[jax.ShapeDtypeStruct(x.shape, F32),
                   jax.ShapeDtypeStruct((b, POOL_HIST, w_a), F32),
                   jax.ShapeDtypeStruct(sq.shape, F32)],
        scratch_shapes=[pltpu.VMEM((rows, w_a), F32), pltpu.VMEM((rows, w_a), F32),
                        pltpu.VMEM((rows, w_b), F32), pltpu.VMEM((rows, w_b), F32),
                        pltpu.VMEM((rows, w_b), F32),
                        pltpu.VMEM((rows, w_a + w_b), BF16),
                        pltpu.VMEM((POOL_HIST, w_a), F32)],
        compiler_params=_params(2),
        name="even_prompt",
    )(x, norm_g, w_in, maps, pscale, sgw, sgbias, sgg, w_out, sq, cache_k, cache_v)


def _odd_prompt_kernel(x_ref, g_ref, w_in_ref, cw_ref, w_out_ref, sq_ref, sk_ref, sv_ref,
                       xo_ref, tail_ref, so_ref, y_ref, hist_ref, *, col_slab):
    rows, w_c = y_ref.shape
    s = pl.program_id(1)

    @pl.when(s == 0)
    def _():
        hist_ref[...] = jnp.zeros_like(hist_ref)

    x = x_ref[0]
    h = _rms(x, g_ref[0]).astype(BF16)
    for c0 in range(0, w_c, col_slab):
        cols = slice(c0, c0 + col_slab)
        zb = [_dot(h, w_in_ref[0, :, part * w_c + c0:part * w_c + c0 + col_slab]) for part in range(4)]
        e = zb[1] * zb[2]
        ext = jnp.concatenate([hist_ref[:, cols], e], axis=0)
        y = (cw_ref[0, 0:1, cols] * pltpu.roll(ext, 2, 0)[CONV_HIST:]
             + cw_ref[0, 1:2, cols] * pltpu.roll(ext, 1, 0)[CONV_HIST:]
             + cw_ref[0, 2:3, cols] * e)
        tail = e[rows - CONV_HIST:]
        hist_ref[:, cols] = tail
        tail_ref[0, :, cols] = tail
        y_ref[:, cols] = (zb[0] * y * _silu(zb[3])).astype(BF16)
    xo_ref[0] = x + _dot(y_ref[...], w_out_ref[0])
    _sample_attention(sq_ref, sk_ref, sv_ref, so_ref)


def _odd_prompt(x, layer, j, norm_g, w_in, conv_w, w_out, sq, cache_k, cache_v):
    b, seq, d = x.shape
    w_c = conv_w.shape[-1]
    rows = PROMPT_ROWS
    col_slab = 512
    xspec = pl.BlockSpec((1, rows, d), lambda i, s: (i, s, 0))
    qspec, kvspec = _sample_attention_specs(sq, cache_k, layer, (b, seq // rows))
    return pl.pallas_call(
        functools.partial(_odd_prompt_kernel, col_slab=col_slab),
        grid=(b, seq // rows),
        in_specs=[xspec, _layer_block(norm_g, layer), _layer_block(w_in, 0), _layer_block(conv_w, j),
                  _layer_block(w_out, 0), qspec, kvspec, kvspec],
        out_specs=[xspec, pl.BlockSpec((1, CONV_HIST, w_c), lambda i, s: (i, 0, 0)), qspec],
        out_shape=[jax.ShapeDtypeStruct(x.shape, F32),
                   jax.ShapeDtypeStruct((b, CONV_HIST, w_c), F32),
                   jax.ShapeDtypeStruct(sq.shape, F32)],
        scratch_shapes=[pltpu.VMEM((rows, w_c), BF16),
                        pltpu.VMEM((CONV_HIST, w_c), F32)],
        compiler_params=_params(2),
        name="odd_prompt",
    )(x, norm_g, w_in, conv_w, w_out, sq, cache_k, cache_v)


def _attn_prompt_kernel(*refs, final, n_cast):
    x_ref, g_ref, wq_ref, kt_ref, v_ref, wo_ref, gf_ref = refs[:7]
    cast_src = refs[7:7 + n_cast]
    xo_ref = refs[7 + n_cast]
    cast_dst = refs[8 + n_cast:8 + 2 * n_cast]
    o_ref = refs[8 + 2 * n_cast]
    for src, dst in zip(cast_src, cast_dst):
        dst[...] = src[...].astype(BF16)
    d = x_ref.shape[-1]
    hd = d // XA_HEADS
    x = x_ref[0]
    h = _rms(x, g_ref[0]).astype(BF16)
    q = _dot(h, wq_ref[0]).astype(BF16)
    heads = [slice(hh * hd, (hh + 1) * hd) for hh in range(XA_HEADS)]
    scores = [_dot(q[:, cols], kt_ref[0, 0, cols, :]) * (hd ** -0.5) for cols in heads]
    for cols, s in zip(heads, scores):
        e = jnp.exp(s - jnp.max(s, axis=-1, keepdims=True))
        p = e / jnp.sum(e, axis=-1, keepdims=True)
        o_ref[:, cols] = _dot(p.astype(BF16), v_ref[0, 0, :, cols]).astype(BF16)
    xn = x + _dot(o_ref[...], wo_ref[0])
    if final:
        xn = _rms(xn, gf_ref[...])
    xo_ref[0] = xn


def _attn_prompt(x, layer, norm_g, w_q, kt, vb, w_o, final_g, final, to_cast):
    b, seq, d = x.shape
    m = vb.shape[2]
    rows = ATTN_ROWS
    n_steps = b * (seq // rows)
    xspec = pl.BlockSpec((1, rows, d), lambda i, s: (i, s, 0))
    step = lambda i, s: i * (seq // rows) + s
    cast_in, cast_out, cast_shapes = [], [], []
    for w, l in to_cast:
        _, k, n = w.shape
        cast_in.append(pl.BlockSpec((1, k // n_steps, n), lambda i, s, l=l: (l, step(i, s), 0)))
        cast_out.append(pl.BlockSpec((1, k // n_steps, n), lambda i, s: (0, step(i, s), 0)))
        cast_shapes.append(jax.ShapeDtypeStruct((1, k, n), BF16))
    outs = pl.pallas_call(
        functools.partial(_attn_prompt_kernel, final=final, n_cast=len(to_cast)),
        grid=(b, seq // rows),
        in_specs=[xspec, _layer_block(norm_g, layer), _layer_block(w_q, 0),
                  pl.BlockSpec((1, 1, d, m), lambda i, s: (layer, i, 0, 0)),
                  pl.BlockSpec((1, 1, m, d), lambda i, s: (layer, i, 0, 0)),
                  _layer_block(w_o, 0),
                  pl.BlockSpec((1, d), lambda i, s: (0, 0), pipeline_mode=pl.Buffered(1))] + cast_in,
        out_specs=[xspec] + cast_out,
        out_shape=[jax.ShapeDtypeStruct(x.shape, F32)] + cast_shapes,
        scratch_shapes=[pltpu.VMEM((rows, d), BF16)],
        compiler_params=_params(2),
        name="attn_prompt",
    )(x, norm_g, w_q, kt, vb, w_o, final_g, *[w for w, _ in to_cast])
    return outs[0], outs[1:]


def _sample_enter(refs, first):
    if first:
        return refs[0][...], refs[1:]
    xs_ref, o_ref, wo_ref = refs[:3]
    return _sample_residual(xs_ref[...], o_ref, wo_ref), refs[3:]


def _sample_residual(xs, o_ref, wo_ref):
    n, d = xs.shape
    rows = o_ref.shape[0] // n
    hd = d // XA_HEADS
    acc = xs
    for r in range(rows):
        f = _tile_row_feature(r, hd)
        o_r = o_ref[pl.ds(r, n, stride=rows), :].astype(BF16)
        acc = acc + _dot(o_r, wo_ref[0, f:f + LANES, :])
    return acc


def _sample_queries(xs, g_ref, wq_ref, q_ref):
    q = _dot(_rms(xs, g_ref[0]).astype(BF16), wq_ref[0])
    q_ref[...] = _to_tile_rows(q, q_ref.shape[1])


def _prev_attention_specs(o_rows, w_o_prev):
    return [pl.BlockSpec(o_rows.shape, lambda i: (0, 0)), _layer_block(w_o_prev, 0)]


def _even_sample_kernel(*refs, pos0, first):
    xs, refs = _sample_enter(refs, first)
    (st_ref, g_ref, w_in_ref, maps_ref, pscale_ref, w00_ref, b0_ref, sgg_ref, w_out_ref, gx_ref, wq_ref,
     xo_ref, pool_ref, vn_ref, q_ref, y_ref) = refs
    w_a, w_b = pscale_ref.shape[-1], sgg_ref.shape[-1]
    gw = w_a // POOL_GROUPS
    h = _rms(xs, g_ref[0]).astype(BF16)
    xa = _dot(h, w_in_ref[0, :, 0:w_a])
    ga = _dot(h, w_in_ref[0, :, w_a:2 * w_a])
    u = _dot(h, w_in_ref[0, :, 2 * w_a:2 * w_a + w_b])
    v = _dot(h, w_in_ref[0, :, 2 * w_a + w_b:2 * w_a + 2 * w_b])
    gb = _dot(h, w_in_ref[0, :, 2 * w_a + 2 * w_b:])

    for g, w in enumerate(POOL_WINDOWS):
        cols = slice(g * gw, (g + 1) * gw)
        acc = xa[:, cols]
        for k in range(POOL_CTX - (w - 1), POOL_CTX):
            acc = acc + st_ref[0, k, :, cols]
        pooled = acc / float(min(pos0 + 1, w)) - xa[:, cols]
        ya = _dot(pooled.astype(BF16), maps_ref[0, g]) * pscale_ref[0, :, cols] * _silu(ga[:, cols])
        y_ref[:, cols] = ya.astype(BF16)
    pool_ref[0, 0:POOL_CTX - 1] = st_ref[0, 1:POOL_CTX]
    pool_ref[0, POOL_CTX - 1] = xa

    vn = _rms(v, sgg_ref[0])
    vn_ref[0] = vn
    mixed = w00_ref[0] * vn + b0_ref[0]
    y_ref[:, w_a:] = (u * mixed * _silu(gb)).astype(BF16)
    xn = xs + _dot(y_ref[...], w_out_ref[0])
    xo_ref[...] = xn
    _sample_queries(xn, gx_ref, wq_ref, q_ref)


def _even_sample(xs, prev, layer, j, state, norm_g, w_in, maps, pscale, w00, b0, sgg, w_out,
                 norm_xg, w_q, w_o, pos0):
    n, d = xs.shape
    _, ctx, _, w_a = state.shape
    w_b = sgg.shape[-1]
    whole = pl.BlockSpec((n, d), lambda i: (0, 0))
    qshape = (n, d // LANES, LANES)
    first = prev is None
    head_specs = [whole] + ([] if first else _prev_attention_specs(prev, w_o))
    head_args = (xs,) if first else (xs, prev, w_o)
    return pl.pallas_call(
        functools.partial(_even_sample_kernel, pos0=pos0, first=first),
        grid=(1,),
        in_specs=head_specs + [
            _layer_block(state, j), _layer_block(norm_g, layer), _layer_block(w_in, 0),
            _layer_block(maps, 0), _layer_block(pscale, j), _layer_block(w00, j), _layer_block(b0, j),
            _layer_block(sgg, j), _layer_block(w_out, 0), _layer_block(norm_xg, layer),
            _layer_block(w_q, 0)],
        out_specs=[whole, pl.BlockSpec((1, ctx, n, w_a), lambda i: (0, 0, 0, 0)),
                   pl.BlockSpec((1, n, w_b), lambda i: (0, 0, 0)),
                   pl.BlockSpec(qshape, lambda i: (0, 0, 0))],
        out_shape=[jax.ShapeDtypeStruct((n, d), F32),
                   jax.ShapeDtypeStruct((1, ctx, n, w_a), F32),
                   jax.ShapeDtypeStruct((1, n, w_b), F32),
                   jax.ShapeDtypeStruct(qshape, F32)],
        scratch_shapes=[pltpu.VMEM((n, w_a + w_b), BF16)],
        compiler_params=_params(1),
        name="even_sample",
    )(*head_args, state, norm_g, w_in, maps, pscale, w00, b0, sgg, w_out, norm_xg, w_q)


def _odd_sample_kernel(*refs, first):
    xs, refs = _sample_enter(refs, first)
    (st_ref, g_ref, w_in_ref, cw_ref, w_out_ref, gx_ref, wq_ref,
     xo_ref, conv_ref, q_ref, y_ref) = refs
    w_c = cw_ref.shape[-1]
    h = _rms(xs, g_ref[0]).astype(BF16)
    bg = _dot(h, w_in_ref[0, :, 0:w_c])
    cg = _dot(h, w_in_ref[0, :, w_c:2 * w_c])
    xc = _dot(h, w_in_ref[0, :, 2 * w_c:3 * w_c])
    gate = _dot(h, w_in_ref[0, :, 3 * w_c:])
    e = cg * xc
    y = (cw_ref[0, 0:1, :] * st_ref[:, 0:w_c] + cw_ref[0, 1:2, :] * st_ref[:, w_c:]
         + cw_ref[0, 2:3, :] * e)
    conv_ref[:, 0:w_c] = st_ref[:, w_c:]
    conv_ref[:, w_c:] = e
    y_ref[...] = (bg * y * _silu(gate)).astype(BF16)
    xn = xs + _dot(y_ref[...], w_out_ref[0])
    xo_ref[...] = xn
    _sample_queries(xn, gx_ref, wq_ref, q_ref)


def _odd_sample(xs, prev, layer, j, state, norm_g, w_in, conv_w, w_out, norm_xg, w_q, w_o):
    n, d = xs.shape
    w_c = conv_w.shape[-1]
    whole = pl.BlockSpec((n, d), lambda i: (0, 0))
    stspec = pl.BlockSpec((n, CONV_CTX * w_c), lambda i: (0, 0))
    qshape = (n, d // LANES, LANES)
    first = prev is None
    head_specs = [whole] + ([] if first else _prev_attention_specs(prev, w_o))
    head_args = (xs,) if first else (xs, prev, w_o)
    return pl.pallas_call(
        functools.partial(_odd_sample_kernel, first=first),
        grid=(1,),
        in_specs=head_specs + [
            stspec, _layer_block(norm_g, layer), _layer_block(w_in, 0), _layer_block(conv_w, j),
            _layer_block(w_out, 0), _layer_block(norm_xg, layer), _layer_block(w_q, 0)],
        out_specs=[whole, stspec, pl.BlockSpec(qshape, lambda i: (0, 0, 0))],
        out_shape=[jax.ShapeDtypeStruct((n, d), F32),
                   jax.ShapeDtypeStruct((n, CONV_CTX * w_c), F32),
                   jax.ShapeDtypeStruct(qshape, F32)],
        scratch_shapes=[pltpu.VMEM((n, w_c), BF16)],
        compiler_params=_params(1),
        name="odd_sample",
    )(*head_args, state, norm_g, w_in, conv_w, w_out, norm_xg, w_q)


def _sample_finish_kernel(xs_ref, o_ref, wo_ref, gf_ref, y_ref):
    y_ref[...] = _rms(_sample_residual(xs_ref[...], o_ref, wo_ref), gf_ref[...])


def _sample_finish(xs, prev, w_o, final_g):
    n, d = xs.shape
    whole = pl.BlockSpec((n, d), lambda i: (0, 0))
    return pl.pallas_call(
        _sample_finish_kernel,
        grid=(1,),
        in_specs=[whole] + _prev_attention_specs(prev, w_o)
                 + [pl.BlockSpec((1, d), lambda i: (0, 0))],
        out_specs=whole,
        out_shape=jax.ShapeDtypeStruct((n, d), F32),
        compiler_params=_params(1),
        name="sample_finish",
    )(xs, prev, w_o, final_g)


def kernel(x_prompt, x_sample, mem_prompt, state_pool, state_conv, cache_mem_k, cache_mem_v, norm_mix_g, norm_xattn_g, norm_mem_g, w_in_ab, pool_maps, pool_scale, sgu_w, sgu_b, sgu_g, w_out_ab, w_in_c, conv_w, w_out_c, w_q, w_k, w_v, w_o, norm_final_g):
    depth, d = norm_mix_g.shape
    n_s, dec_seq, _ = x_sample.shape
    n_even, n_odd = pool_scale.shape[0], conv_w.shape[0]
    w_a, w_b, w_c = pool_scale.shape[-1], sgu_g.shape[-1], conv_w.shape[-1]
    hd = d // XA_HEADS
    assert dec_seq == 1, "sample group is one new token per sequence"
    assert PAST_LEN % CHUNK == 0, "the sample token must open a spatial-gating chunk"
    assert (hd // LANES) * XA_HEADS == SUBLANES, "one memory token per (8, 128) register tile"

    maps_rows = pool_maps.reshape(n_even, w_a, w_a // POOL_GROUPS)

    def matmul_weights(i):
        j = i // 2
        mixer = [(w_in_ab, j), (w_out_ab, j), (maps_rows, j)] if i % 2 == 0 else [(w_in_c, j), (w_out_c, j)]
        return mixer + [(w_q, i), (w_o, i)]

    wts = [w[l:l + 1].astype(BF16) for w, l in matmul_weights(0)]
    row = lambda a: a.reshape(a.shape[0], 1, a.shape[-1])
    g_mix, g_xattn, g_mem = row(norm_mix_g), row(norm_xattn_g), row(norm_mem_g)
    pscale, sgg = row(pool_scale), row(sgu_g)
    g_final = norm_final_g.reshape(1, d)
    sw = w_b // SGU_GROUPS
    sgbias = jnp.repeat(jnp.swapaxes(sgu_b, 1, 2), sw, axis=2)
    w00 = row(jnp.repeat(sgu_w[:, :, 0, 0], sw, axis=1))
    b0 = row(jnp.repeat(sgu_b[:, :, 0], sw, axis=1))

    k_p, v_p, kt_p, vb_p = _memory_kv(mem_prompt, g_mem, w_k, w_v)
    cache_k = _lane_tiled(cache_mem_k)
    cache_v = _lane_tiled(cache_mem_v)
    pool_state = jnp.swapaxes(state_pool, 1, 2)
    conv_state = state_conv.reshape(n_odd, n_s, CONV_CTX * w_c)

    xp = x_prompt
    xs = x_sample.reshape(n_s, d)
    so = wo_prev = None
    pool_p, pool_s, conv_p, conv_s, vrows_s = [], [], [], [], []
    for i in range(depth):
        j = i // 2
        if i % 2 == 0:
            w_in_b, w_out_b, maps_b, wq_b, wo_b = wts
            maps_b = maps_b.reshape(1, POOL_GROUPS, w_a // POOL_GROUPS, w_a // POOL_GROUPS)
            xs, pool_new, vn, sq = _even_sample(xs, so, i, j, pool_state, g_mix, w_in_b, maps_b,
                                                pscale, w00, b0, sgg, w_out_b, g_xattn, wq_b,
                                                wo_prev, PAST_LEN)
            pool_s.append(pool_new)
            vrows_s.append(vn)
            xp, tail, so = _even_prompt(xp, i, j, g_mix, w_in_b, maps_b, pscale, sgu_w, sgbias, sgg,
                                        w_out_b, sq, cache_k, cache_v)
            pool_p.append(tail[:, POOL_HIST - POOL_CTX:])
        else:
            w_in_b, w_out_b, wq_b, wo_b = wts
            xs, conv_new, sq = _odd_sample(xs, so, i, j, conv_state[j], g_mix, w_in_b, conv_w,
                                           w_out_b, g_xattn, wq_b, wo_prev)
            conv_s.append(conv_new.reshape(n_s, CONV_CTX, w_c))
            xp, tail, so = _odd_prompt(xp, i, j, g_mix, w_in_b, conv_w, w_out_b, sq, cache_k, cache_v)
            conv_p.append(tail[:, CONV_HIST - CONV_CTX:])
        so = so.reshape(n_s * so.shape[1], LANES)
        wo_prev = wo_b
        xp, wts = _attn_prompt(xp, i, g_xattn, wq_b, kt_p, vb_p, wo_b, g_final, i == depth - 1,
                               matmul_weights(i + 1) if i + 1 < depth else [])
    ys = _sample_finish(xs, so, wo_prev, g_final)

    new_pool_s = jnp.swapaxes(jnp.concatenate(pool_s, axis=0), 1, 2)
    new_vrows = jnp.concatenate(vrows_s, axis=0).reshape(n_even, n_s, 1, w_b)
    return (xp, ys.reshape(n_s, 1, d), jnp.stack(pool_p), new_pool_s, jnp.stack(conv_p),
            jnp.stack(conv_s), new_vrows, _from_lane_tiled(k_p), _from_lane_tiled(v_p))
```

```python
import functools

import jax
import jax.numpy as jnp
from jax import lax
from jax.experimental import pallas as pl
from jax.experimental.pallas import tpu as pltpu

POOL_WINDOWS = (2, 4, 8, 16)
POOL_GROUPS = len(POOL_WINDOWS)
POOL_CTX = max(POOL_WINDOWS) - 1
SGU_GROUPS = 4
CHUNK = 128
CONV_WIDTH = 3
CONV_CTX = CONV_WIDTH - 1
XA_HEADS = 4
EPS = 1e-6
PAST_LEN = 16384

V7X_VMEM_LIMIT_BYTES = 60 * 1024 * 1024
SUBLANES = 8
LANES = 128
POOL_HIST = 16
CONV_HIST = 8

PROMPT_ROWS = 512
ATTN_ROWS = 1024
MEMKV_BATCH = 2
SAMPLE_TOKEN_BLOCK = 32

F32 = jnp.float32
BF16 = jnp.bfloat16


def _rms(x, g):
    return x * lax.rsqrt(jnp.mean(x * x, axis=-1, keepdims=True) + EPS) * g


def _silu(x):
    return x * jax.nn.sigmoid(x)


def _dot(a, b):
    return jnp.dot(a, b, preferred_element_type=F32)


def _layer_block(arr, layer):
    index = (layer,) + (0,) * (arr.ndim - 1)
    return pl.BlockSpec((1,) + arr.shape[1:], lambda *_: index, pipeline_mode=pl.Buffered(1))


def _params(n_grid):
    return pltpu.CompilerParams(dimension_semantics=("arbitrary",) * n_grid,
                                vmem_limit_bytes=V7X_VMEM_LIMIT_BYTES)


def _tile_row_feature(r, hd):
    return (r % XA_HEADS) * hd + (r // XA_HEADS) * LANES


def _lane_tiled(cache):
    l, b, m, h, hd = cache.shape
    t = hd // LANES
    return cache.reshape(l, b, m, h, t, LANES).transpose(0, 1, 2, 4, 3, 5).reshape(l, b, m, t * h, LANES)


def _from_lane_tiled(tiled):
    l, b, m, r, lanes = tiled.shape
    t = r // XA_HEADS
    return tiled.reshape(l, b, m, t, XA_HEADS, lanes).transpose(0, 1, 2, 4, 3, 5).reshape(
        l, b, m, XA_HEADS, t * lanes)


def _to_tile_rows(a, rows):
    hd = a.shape[1] // XA_HEADS
    pieces = [a[:, _tile_row_feature(r, hd):_tile_row_feature(r, hd) + LANES] for r in range(rows)]
    return jnp.swapaxes(jnp.stack(pieces, axis=0), 0, 1)


def _cast_specs(to_cast, n_steps, step):
    ins, outs, shapes = [], [], []
    for w, l in to_cast:
        _, k, n = w.shape
        ins.append(pl.BlockSpec((1, k // n_steps, n), lambda i, s, l=l: (l, step(i, s), 0)))
        outs.append(pl.BlockSpec((1, k // n_steps, n), lambda i, s: (0, step(i, s), 0)))
        shapes.append(jax.ShapeDtypeStruct((1, k, n), BF16))
    return ins, outs, shapes


def _cast_slabs(srcs, dsts):
    for src, dst in zip(srcs, dsts):
        dst[...] = src[...].astype(BF16)


def _memkv_kernel(*refs, n_cast):
    mem_ref, g_ref, wk_ref, wv_ref = refs[:4]
    k_ref, v_ref, kt_ref, vb_ref = refs[4 + n_cast:8 + n_cast]
    _cast_slabs(refs[4:4 + n_cast], refs[8 + n_cast:])
    nb, m, d = mem_ref.shape
    rows = k_ref.shape[3]
    mn = _rms(mem_ref[...].reshape(nb * m, d), g_ref[0]).astype(BF16)
    k = _dot(mn, wk_ref[0].astype(BF16))
    v = _dot(mn, wv_ref[0].astype(BF16))
    for i in range(nb):
        tok = slice(i * m, (i + 1) * m)
        k_ref[0, i] = _to_tile_rows(k[tok], rows)
        v_ref[0, i] = _to_tile_rows(v[tok], rows)
        kt_ref[0, i] = k[tok].T.astype(BF16)
        vb_ref[0, i] = v[tok].astype(BF16)


def _memory_kv(mem, norm_g, w_k, w_v, to_cast):
    depth, _, d = norm_g.shape
    b, m, _ = mem.shape
    rows = d // LANES
    nb = MEMKV_BATCH
    wspec = pl.BlockSpec((1, d, d), lambda l, i: (l, 0, 0))
    tiled = pl.BlockSpec((1, nb, m, rows, LANES), lambda l, i: (l, i, 0, 0, 0))
    cast_in, cast_out, cast_shapes = _cast_specs(to_cast, depth * (b // nb), lambda l, i: l * (b // nb) + i)
    outs = pl.pallas_call(
        functools.partial(_memkv_kernel, n_cast=len(to_cast)),
        grid=(depth, b // nb),
        in_specs=[pl.BlockSpec((nb, m, d), lambda l, i: (i, 0, 0)),
                  pl.BlockSpec((1, 1, d), lambda l, i: (l, 0, 0)),
                  wspec, wspec] + cast_in,
        out_specs=[tiled, tiled,
                   pl.BlockSpec((1, nb, d, m), lambda l, i: (l, i, 0, 0)),
                   pl.BlockSpec((1, nb, m, d), lambda l, i: (l, i, 0, 0))] + cast_out,
        out_shape=[jax.ShapeDtypeStruct((depth, b, m, rows, LANES), F32),
                   jax.ShapeDtypeStruct((depth, b, m, rows, LANES), F32),
                   jax.ShapeDtypeStruct((depth, b, d, m), BF16),
                   jax.ShapeDtypeStruct((depth, b, m, d), BF16)] + cast_shapes,
        compiler_params=_params(2),
        name="memory_kv",
    )(mem, norm_g, w_k, w_v, *[w for w, _ in to_cast])
    return outs[:4], outs[4:]


def _sample_attention(q_ref, k_ref, v_ref, o_ref):
    n, rows, lanes = q_ref.shape
    m = k_ref.shape[2]
    hd = rows * lanes // XA_HEADS
    for i in range(n):
        qt = q_ref[i] * (hd ** -0.5)
        mx = den = acc = None
        for t0 in range(0, m, SAMPLE_TOKEN_BLOCK):
            tok = slice(t0, t0 + SAMPLE_TOKEN_BLOCK)
            part = jnp.sum(k_ref[0, i, tok] * qt[None], axis=-1, keepdims=True)
            s = part + pltpu.roll(part, XA_HEADS, 1)
            block_max = jnp.max(s, axis=0)
            new_mx = block_max if mx is None else jnp.maximum(mx, block_max)
            e = jnp.exp(s - new_mx)
            block_den = jnp.sum(e, axis=0)
            block_acc = jnp.sum(e * v_ref[0, i, tok], axis=0)
            if mx is None:
                den, acc = block_den, block_acc
            else:
                alpha = jnp.exp(mx - new_mx)
                den = den * alpha + block_den
                acc = acc * alpha + block_acc
            mx = new_mx
        o_ref[i] = acc / den


def _sample_attention_specs(q_tiles, cache_k, layer, n_steps):
    n, rows, lanes = q_tiles.shape
    m = cache_k.shape[2]
    per_step = n // (n_steps[0] * n_steps[1])
    step = lambda i, s: i * n_steps[1] + s
    qspec = pl.BlockSpec((per_step, rows, lanes), lambda i, s: (step(i, s), 0, 0))
    kvspec = pl.BlockSpec((1, per_step, m, rows, lanes), lambda i, s: (layer, step(i, s), 0, 0, 0))
    return qspec, kvspec


def _even_prompt_kernel(x_ref, g_ref, w_in_ref, maps_ref, pscale_ref, sgw_ref, sgbias_ref, sgg_ref,
                        w_out_ref, sq_ref, sk_ref, sv_ref, xo_ref, tail_ref, so_ref,
                        xa_ref, ga_ref, u_ref, v_ref, gb_ref, y_ref, hist_ref):
    rows, w_a = y_ref.shape[0], pscale_ref.shape[-1]
    w_b = sgg_ref.shape[-1]
    gw = w_a // POOL_GROUPS
    sw = w_b // SGU_GROUPS
    s = pl.program_id(1)

    @pl.when(s == 0)
    def _():
        hist_ref[...] = jnp.zeros_like(hist_ref)

    x = x_ref[0]
    h = _rms(x, g_ref[0]).astype(BF16)
    xa_ref[...] = _dot(h, w_in_ref[0, :, 0:w_a])
    v_ref[...] = _dot(h, w_in_ref[0, :, 2 * w_a + w_b:2 * w_a + 2 * w_b])

    pos = s * rows + lax.broadcasted_iota(jnp.int32, (rows, gw), 0)
    pooled = []
    for g, w in enumerate(POOL_WINDOWS):
        cols = slice(g * gw, (g + 1) * gw)
        xa = xa_ref[:, cols]
        acc = jnp.concatenate([hist_ref[:, cols], xa], axis=0)
        shift = 1
        while shift < w:
            acc = acc + pltpu.roll(acc, shift, 0)
            shift *= 2
        cnt = jnp.minimum(pos + 1, w).astype(F32)
        pooled.append((acc[POOL_HIST:] / cnt - xa).astype(BF16))
    tail = xa_ref[rows - POOL_HIST:, :]
    hist_ref[...] = tail
    tail_ref[0] = tail

    ga_ref[...] = _dot(h, w_in_ref[0, :, w_a:2 * w_a])
    for g in range(POOL_GROUPS):
        cols = slice(g * gw, (g + 1) * gw)
        ya = _dot(pooled[g], maps_ref[0, g]) * pscale_ref[0, :, cols] * _silu(ga_ref[:, cols])
        y_ref[:, cols] = ya.astype(BF16)

    u_ref[...] = _dot(h, w_in_ref[0, :, 2 * w_a:2 * w_a + w_b])
    gb_ref[...] = _dot(h, w_in_ref[0, :, 2 * w_a + 2 * w_b:])
    out = x + _dot(y_ref[:, 0:w_a], w_out_ref[0, 0:w_a, :])

    vn = _rms(v_ref[...], sgg_ref[0]).astype(BF16)
    causal = (lax.broadcasted_iota(jnp.int32, (CHUNK, CHUNK), 0)
              >= lax.broadcasted_iota(jnp.int32, (CHUNK, CHUNK), 1))
    for g in range(SGU_GROUPS):
        cols = slice(g * sw, (g + 1) * sw)
        wg = jnp.where(causal, sgw_ref[0, g], 0.0).astype(BF16)
        for c in range(rows // CHUNK):
            rs = slice(c * CHUNK, (c + 1) * CHUNK)
            mixed = _dot(wg, vn[rs, cols]) + sgbias_ref[0, :, cols]
            y_ref[rs, w_a + g * sw:w_a + (g + 1) * sw] = (
                u_ref[rs, cols] * mixed * _silu(gb_ref[rs, cols])).astype(BF16)

    xo_ref[0] = out + _dot(y_ref[:, w_a:], w_out_ref[0, w_a:, :])
    _sample_attention(sq_ref, sk_ref, sv_ref, so_ref)


def _even_prompt(x, layer, j, norm_g, w_in, maps, pscale, sgw, sgbias, sgg, w_out, sq, cache_k, cache_v):
    b, seq, d = x.shape
    w_a, w_b = pscale.shape[-1], sgg.shape[-1]
    rows = PROMPT_ROWS
    xspec = pl.BlockSpec((1, rows, d), lambda i, s: (i, s, 0))
    qspec, kvspec = _sample_attention_specs(sq, cache_k, layer, (b, seq // rows))
    return pl.pallas_call(
        _even_prompt_kernel,
        grid=(b, seq // rows),
        in_specs=[xspec, _layer_block(norm_g, layer), _layer_block(w_in, 0), _layer_block(maps, 0),
                  _layer_block(pscale, j), _layer_block(sgw, j), _layer_block(sgbias, j),
                  _layer_block(sgg, j), _layer_block(w_out, 0), qspec, kvspec, kvspec],
        out_specs=[xspec, pl.BlockSpec((1, POOL_HIST, w_a), lambda i, s: (i, 0, 0)), qspec],
        out_shape=[jax.ShapeDtypeStruct(x.shape, F32),
                   jax.ShapeDtypeStruct((b, POOL_HIST, w_a), F32),
                   jax.ShapeDtypeStruct(sq.shape, F32)],
        scratch_shapes=[pltpu.VMEM((rows, w_a), F32), pltpu.VMEM((rows, w_a), F32),
                        pltpu.VMEM((rows, w_b), F32), pltpu.VMEM((rows, w_b), F32),
                        pltpu.VMEM((rows, w_b), F32),
                        pltpu.VMEM((rows, w_a + w_b), BF16),
                        pltpu.VMEM((POOL_HIST, w_a), F32)],
        compiler_params=_params(2),
        name="even_prompt",
    )(x, norm_g, w_in, maps, pscale, sgw, sgbias, sgg, w_out, sq, cache_k, cache_v)


def _odd_prompt_kernel(x_ref, g_ref, w_in_ref, cw_ref, w_out_ref, sq_ref, sk_ref, sv_ref,
                       xo_ref, tail_ref, so_ref, y_ref, hist_ref, *, col_slab):
    rows, w_c = y_ref.shape
    s = pl.program_id(1)

    @pl.when(s == 0)
    def _():
        hist_ref[...] = jnp.zeros_like(hist_ref)

    x = x_ref[0]
    h = _rms(x, g_ref[0]).astype(BF16)
    for c0 in range(0, w_c, col_slab):
        cols = slice(c0, c0 + col_slab)
        zb = [_dot(h, w_in_ref[0, :, part * w_c + c0:part * w_c + c0 + col_slab]) for part in range(4)]
        e = zb[1] * zb[2]
        ext = jnp.concatenate([hist_ref[:, cols], e], axis=0)
        y = (cw_ref[0, 0:1, cols] * pltpu.roll(ext, 2, 0)[CONV_HIST:]
             + cw_ref[0, 1:2, cols] * pltpu.roll(ext, 1, 0)[CONV_HIST:]
             + cw_ref[0, 2:3, cols] * e)
        tail = e[rows - CONV_HIST:]
        hist_ref[:, cols] = tail
        tail_ref[0, :, cols] = tail
        y_ref[:, cols] = (zb[0] * y * _silu(zb[3])).astype(BF16)
    xo_ref[0] = x + _dot(y_ref[...], w_out_ref[0])
    _sample_attention(sq_ref, sk_ref, sv_ref, so_ref)


def _odd_prompt(x, layer, j, norm_g, w_in, conv_w, w_out, sq, cache_k, cache_v):
    b, seq, d = x.shape
    w_c = conv_w.shape[-1]
    rows = PROMPT_ROWS
    col_slab = 512
    xspec = pl.BlockSpec((1, rows, d), lambda i, s: (i, s, 0))
    qspec, kvspec = _sample_attention_specs(sq, cache_k, layer, (b, seq // rows))
    return pl.pallas_call(
        functools.partial(_odd_prompt_kernel, col_slab=col_slab),
        grid=(b, seq // rows),
        in_specs=[xspec, _layer_block(norm_g, layer), _layer_block(w_in, 0), _layer_block(conv_w, j),
                  _layer_block(w_out, 0), qspec, kvspec, kvspec],
        out_specs=[xspec, pl.BlockSpec((1, CONV_HIST, w_c), lambda i, s: (i, 0, 0)), qspec],
        out_shape=[jax.ShapeDtypeStruct(x.shape, F32),
                   jax.ShapeDtypeStruct((b, CONV_HIST, w_c), F32),
                   jax.ShapeDtypeStruct(sq.shape, F32)],
        scratch_shapes=[pltpu.VMEM((rows, w_c), BF16),
                        pltpu.VMEM((CONV_HIST, w_c), F32)],
        compiler_params=_params(2),
        name="odd_prompt",
    )(x, norm_g, w_in, conv_w, w_out, sq, cache_k, cache_v)


def _attn_prompt_kernel(*refs, final, n_cast):
    x_ref, g_ref, wq_ref, kt_ref, v_ref, wo_ref, gf_ref = refs[:7]
    cast_src = refs[7:7 + n_cast]
    xo_ref = refs[7 + n_cast]
    cast_dst = refs[8 + n_cast:8 + 2 * n_cast]
    o_ref = refs[8 + 2 * n_cast]
    _cast_slabs(cast_src, cast_dst)
    d = x_ref.shape[-1]
    hd = d // XA_HEADS
    x = x_ref[0]
    h = _rms(x, g_ref[0]).astype(BF16)
    q = _dot(h, wq_ref[0]).astype(BF16)
    heads = [slice(hh * hd, (hh + 1) * hd) for hh in range(XA_HEADS)]
    scores = [_dot(q[:, cols], kt_ref[0, 0, cols, :]) * (hd ** -0.5) for cols in heads]
    for cols, s in zip(heads, scores):
        e = jnp.exp(s - jnp.max(s, axis=-1, keepdims=True))
        p = e / jnp.sum(e, axis=-1, keepdims=True)
        o_ref[:, cols] = _dot(p.astype(BF16), v_ref[0, 0, :, cols]).astype(BF16)
    xn = x + _dot(o_ref[...], wo_ref[0])
    if final:
        xn = _rms(xn, gf_ref[...])
    xo_ref[0] = xn


def _attn_prompt(x, layer, norm_g, w_q, kt, vb, w_o, final_g, final, to_cast):
    b, seq, d = x.shape
    m = vb.shape[2]
    rows = ATTN_ROWS
    n_steps = b * (seq // rows)
    xspec = pl.BlockSpec((1, rows, d), lambda i, s: (i, s, 0))
    cast_in, cast_out, cast_shapes = _cast_specs(to_cast, n_steps, lambda i, s: i * (seq // rows) + s)
    outs = pl.pallas_call(
        functools.partial(_attn_prompt_kernel, final=final, n_cast=len(to_cast)),
        grid=(b, seq // rows),
        in_specs=[xspec, _layer_block(norm_g, layer), _layer_block(w_q, 0),
                  pl.BlockSpec((1, 1, d, m), lambda i, s: (layer, i, 0, 0)),
                  pl.BlockSpec((1, 1, m, d), lambda i, s: (layer, i, 0, 0)),
                  _layer_block(w_o, 0),
                  pl.BlockSpec((1, d), lambda i, s: (0, 0), pipeline_mode=pl.Buffered(1))] + cast_in,
        out_specs=[xspec] + cast_out,
        out_shape=[jax.ShapeDtypeStruct(x.shape, F32)] + cast_shapes,
        scratch_shapes=[pltpu.VMEM((rows, d), BF16)],
        compiler_params=_params(2),
        name="attn_prompt",
    )(x, norm_g, w_q, kt, vb, w_o, final_g, *[w for w, _ in to_cast])
    return outs[0], outs[1:]


def _sample_enter(refs, first):
    if first:
        return refs[0][...], refs[1:]
    xs_ref, o_ref, wo_ref = refs[:3]
    return _sample_residual(xs_ref[...], o_ref, wo_ref), refs[3:]


def _sample_residual(xs, o_ref, wo_ref):
    n, d = xs.shape
    rows = o_ref.shape[0] // n
    hd = d // XA_HEADS
    acc = xs
    for r in range(rows):
        f = _tile_row_feature(r, hd)
        o_r = o_ref[pl.ds(r, n, stride=rows), :].astype(BF16)
        acc = acc + _dot(o_r, wo_ref[0, f:f + LANES, :])
    return acc


def _sample_queries(xs, g_ref, wq_ref, q_ref):
    q = _dot(_rms(xs, g_ref[0]).astype(BF16), wq_ref[0])
    q_ref[...] = _to_tile_rows(q, q_ref.shape[1])


def _prev_attention_specs(o_rows, w_o_prev):
    return [pl.BlockSpec(o_rows.shape, lambda i: (0, 0)), _layer_block(w_o_prev, 0)]


def _even_sample_kernel(*refs, pos0, first):
    xs, refs = _sample_enter(refs, first)
    (st_ref, g_ref, w_in_ref, maps_ref, pscale_ref, w00_ref, b0_ref, sgg_ref, w_out_ref, gx_ref, wq_ref,
     xo_ref, pool_ref, vn_ref, q_ref, y_ref) = refs
    w_a, w_b = pscale_ref.shape[-1], sgg_ref.shape[-1]
    gw = w_a // POOL_GROUPS
    h = _rms(xs, g_ref[0]).astype(BF16)
    xa = _dot(h, w_in_ref[0, :, 0:w_a])
    ga = _dot(h, w_in_ref[0, :, w_a:2 * w_a])
    u = _dot(h, w_in_ref[0, :, 2 * w_a:2 * w_a + w_b])
    v = _dot(h, w_in_ref[0, :, 2 * w_a + w_b:2 * w_a + 2 * w_b])
    gb = _dot(h, w_in_ref[0, :, 2 * w_a + 2 * w_b:])

    for g, w in enumerate(POOL_WINDOWS):
        cols = slice(g * gw, (g + 1) * gw)
        acc = xa[:, cols]
        for k in range(POOL_CTX - (w - 1), POOL_CTX):
            acc = acc + st_ref[0, k, :, cols]
        pooled = acc / float(min(pos0 + 1, w)) - xa[:, cols]
        ya = _dot(pooled.astype(BF16), maps_ref[0, g]) * pscale_ref[0, :, cols] * _silu(ga[:, cols])
        y_ref[:, cols] = ya.astype(BF16)
    pool_ref[0, 0:POOL_CTX - 1] = st_ref[0, 1:POOL_CTX]
    pool_ref[0, POOL_CTX - 1] = xa

    vn = _rms(v, sgg_ref[0])
    vn_ref[0] = vn
    mixed = w00_ref[0] * vn + b0_ref[0]
    y_ref[:, w_a:] = (u * mixed * _silu(gb)).astype(BF16)
    xn = xs + _dot(y_ref[...], w_out_ref[0])
    xo_ref[...] = xn
    _sample_queries(xn, gx_ref, wq_ref, q_ref)


def _even_sample(xs, prev, layer, j, state, norm_g, w_in, maps, pscale, w00, b0, sgg, w_out,
                 norm_xg, w_q, w_o, pos0):
    n, d = xs.shape
    _, ctx, _, w_a = state.shape
    w_b = sgg.shape[-1]
    whole = pl.BlockSpec((n, d), lambda i: (0, 0))
    qshape = (n, d // LANES, LANES)
    first = prev is None
    head_specs = [whole] + ([] if first else _prev_attention_specs(prev, w_o))
    head_args = (xs,) if first else (xs, prev, w_o)
    return pl.pallas_call(
        functools.partial(_even_sample_kernel, pos0=pos0, first=first),
        grid=(1,),
        in_specs=head_specs + [
            _layer_block(state, j), _layer_block(norm_g, layer), _layer_block(w_in, 0),
            _layer_block(maps, 0), _layer_block(pscale, j), _layer_block(w00, j), _layer_block(b0, j),
            _layer_block(sgg, j), _layer_block(w_out, 0), _layer_block(norm_xg, layer),
            _layer_block(w_q, 0)],
        out_specs=[whole, pl.BlockSpec((1, ctx, n, w_a), lambda i: (0, 0, 0, 0)),
                   pl.BlockSpec((1, n, w_b), lambda i: (0, 0, 0)),
                   pl.BlockSpec(qshape, lambda i: (0, 0, 0))],
        out_shape=[jax.ShapeDtypeStruct((n, d), F32),
                   jax.ShapeDtypeStruct((1, ctx, n, w_a), F32),
                   jax.ShapeDtypeStruct((1, n, w_b), F32),
                   jax.ShapeDtypeStruct(qshape, F32)],
        scratch_shapes=[pltpu.VMEM((n, w_a + w_b), BF16)],
        compiler_params=_params(1),
        name="even_sample",
    )(*head_args, state, norm_g, w_in, maps, pscale, w00, b0, sgg, w_out, norm_xg, w_q)


def _odd_sample_kernel(*refs, first):
    xs, refs = _sample_enter(refs, first)
    (st_ref, g_ref, w_in_ref, cw_ref, w_out_ref, gx_ref, wq_ref,
     xo_ref, conv_ref, q_ref, y_ref) = refs
    w_c = cw_ref.shape[-1]
    h = _rms(xs, g_ref[0]).astype(BF16)
    bg = _dot(h, w_in_ref[0, :, 0:w_c])
    cg = _dot(h, w_in_ref[0, :, w_c:2 * w_c])
    xc = _dot(h, w_in_ref[0, :, 2 * w_c:3 * w_c])
    gate = _dot(h, w_in_ref[0, :, 3 * w_c:])
    e = cg * xc
    y = (cw_ref[0, 0:1, :] * st_ref[:, 0:w_c] + cw_ref[0, 1:2, :] * st_ref[:, w_c:]
         + cw_ref[0, 2:3, :] * e)
    conv_ref[:, 0:w_c] = st_ref[:, w_c:]
    conv_ref[:, w_c:] = e
    y_ref[...] = (bg * y * _silu(gate)).astype(BF16)
    xn = xs + _dot(y_ref[...], w_out_ref[0])
    xo_ref[...] = xn
    _sample_queries(xn, gx_ref, wq_ref, q_ref)


def _odd_sample(xs, prev, layer, j, state, norm_g, w_in, conv_w, w_out, norm_xg, w_q, w_o):
    n, d = xs.shape
    w_c = conv_w.shape[-1]
    whole = pl.BlockSpec((n, d), lambda i: (0, 0))
    stspec = pl.BlockSpec((n, CONV_CTX * w_c), lambda i: (0, 0))
    qshape = (n, d // LANES, LANES)
    first = prev is None
    head_specs = [whole] + ([] if first else _prev_attention_specs(prev, w_o))
    head_args = (xs,) if first else (xs, prev, w_o)
    return pl.pallas_call(
        functools.partial(_odd_sample_kernel, first=first),
        grid=(1,),
        in_specs=head_specs + [
            stspec, _layer_block(norm_g, layer), _layer_block(w_in, 0), _layer_block(conv_w, j),
            _layer_block(w_out, 0), _layer_block(norm_xg, layer), _layer_block(w_q, 0)],
        out_specs=[whole, stspec, pl.BlockSpec(qshape, lambda i: (0, 0, 0))],
        out_shape=[jax.ShapeDtypeStruct((n, d), F32),
                   jax.ShapeDtypeStruct((n, CONV_CTX * w_c), F32),
                   jax.ShapeDtypeStruct(qshape, F32)],
        scratch_shapes=[pltpu.VMEM((n, w_c), BF16)],
        compiler_params=_params(1),
        name="odd_sample",
    )(*head_args, state, norm_g, w_in, conv_w, w_out, norm_xg, w_q)


def _sample_finish_kernel(xs_ref, o_ref, wo_ref, gf_ref, y_ref):
    y_ref[...] = _rms(_sample_residual(xs_ref[...], o_ref, wo_ref), gf_ref[...])


def _sample_finish(xs, prev, w_o, final_g):
    n, d = xs.shape
    whole = pl.BlockSpec((n, d), lambda i: (0, 0))
    return pl.pallas_call(
        _sample_finish_kernel,
        grid=(1,),
        in_specs=[whole] + _prev_attention_specs(prev, w_o)
                 + [pl.BlockSpec((1, d), lambda i: (0, 0))],
        out_specs=whole,
        out_shape=jax.ShapeDtypeStruct((n, d), F32),
        compiler_params=_params(1),
        name="sample_finish",
    )(xs, prev, w_o, final_g)


def kernel(x_prompt, x_sample, mem_prompt, state_pool, state_conv, cache_mem_k, cache_mem_v, norm_mix_g, norm_xattn_g, norm_mem_g, w_in_ab, pool_maps, pool_scale, sgu_w, sgu_b, sgu_g, w_out_ab, w_in_c, conv_w, w_out_c, w_q, w_k, w_v, w_o, norm_final_g):
    depth, d = norm_mix_g.shape
    n_s, dec_seq, _ = x_sample.shape
    n_even, n_odd = pool_scale.shape[0], conv_w.shape[0]
    w_a, w_b, w_c = pool_scale.shape[-1], sgu_g.shape[-1], conv_w.shape[-1]
    hd = d // XA_HEADS
    assert dec_seq == 1, "sample group is one new token per sequence"
    assert PAST_LEN % CHUNK == 0, "the sample token must open a spatial-gating chunk"
    assert (hd // LANES) * XA_HEADS == SUBLANES, "one memory token per (8, 128) register tile"

    maps_rows = pool_maps.reshape(n_even, w_a, w_a // POOL_GROUPS)

    def matmul_weights(i):
        j = i // 2
        mixer = [(w_in_ab, j), (w_out_ab, j), (maps_rows, j)] if i % 2 == 0 else [(w_in_c, j), (w_out_c, j)]
        return mixer + [(w_q, i), (w_o, i)]

    row = lambda a: a.reshape(a.shape[0], 1, a.shape[-1])
    g_mix, g_xattn, g_mem = row(norm_mix_g), row(norm_xattn_g), row(norm_mem_g)
    pscale, sgg = row(pool_scale), row(sgu_g)
    g_final = norm_final_g.reshape(1, d)
    sw = w_b // SGU_GROUPS
    sgbias = jnp.repeat(jnp.swapaxes(sgu_b, 1, 2), sw, axis=2)
    w00 = row(jnp.repeat(sgu_w[:, :, 0, 0], sw, axis=1))
    b0 = row(jnp.repeat(sgu_b[:, :, 0], sw, axis=1))

    (k_p, v_p, kt_p, vb_p), wts = _memory_kv(mem_prompt, g_mem, w_k, w_v, matmul_weights(0))
    cache_k = _lane_tiled(cache_mem_k)
    cache_v = _lane_tiled(cache_mem_v)
    pool_state = jnp.swapaxes(state_pool, 1, 2)
    conv_state = state_conv.reshape(n_odd, n_s, CONV_CTX * w_c)

    xp = x_prompt
    xs = x_sample.reshape(n_s, d)
    so = wo_prev = None
    pool_p, pool_s, conv_p, conv_s, vrows_s = [], [], [], [], []
    for i in range(depth):
        j = i // 2
        if i % 2 == 0:
            w_in_b, w_out_b, maps_b, wq_b, wo_b = wts
            maps_b = maps_b.reshape(1, POOL_GROUPS, w_a // POOL_GROUPS, w_a // POOL_GROUPS)
            xs, pool_new, vn, sq = _even_sample(xs, so, i, j, pool_state, g_mix, w_in_b, maps_b,
                                                pscale, w00, b0, sgg, w_out_b, g_xattn, wq_b,
                                                wo_prev, PAST_LEN)
            pool_s.append(pool_new)
            vrows_s.append(vn)
            xp, tail, so = _even_prompt(xp, i, j, g_mix, w_in_b, maps_b, pscale, sgu_w, sgbias, sgg,
                                        w_out_b, sq, cache_k, cache_v)
            pool_p.append(tail[:, POOL_HIST - POOL_CTX:])
        else:
            w_in_b, w_out_b, wq_b, wo_b = wts
            xs, conv_new, sq = _odd_sample(xs, so, i, j, conv_state[j], g_mix, w_in_b, conv_w,
                                           w_out_b, g_xattn, wq_b, wo_prev)
            conv_s.append(conv_new.reshape(n_s, CONV_CTX, w_c))
            xp, tail, so = _odd_prompt(xp, i, j, g_mix, w_in_b, conv_w, w_out_b, sq, cache_k, cache_v)
            conv_p.append(tail[:, CONV_HIST - CONV_CTX:])
        so = so.reshape(n_s * so.shape[1], LANES)
        wo_prev = wo_b
        xp, wts = _attn_prompt(xp, i, g_xattn, wq_b, kt_p, vb_p, wo_b, g_final, i == depth - 1,
                               matmul_weights(i + 1) if i + 1 < depth else [])
    ys = _sample_finish(xs, so, wo_prev, g_final)

    new_pool_s = jnp.swapaxes(jnp.concatenate(pool_s, axis=0), 1, 2)
    new_vrows = jnp.concatenate(vrows_s, axis=0).reshape(n_even, n_s, 1, w_b)
    return (xp, ys.reshape(n_s, 1, d), jnp.stack(pool_p), new_pool_s, jnp.stack(conv_p),
            jnp.stack(conv_s), new_vrows, _from_lane_tiled(k_p), _from_lane_tiled(v_p))
```

```python
import functools

import jax
import jax.numpy as jnp
from jax import lax
from jax.experimental import pallas as pl
from jax.experimental.pallas import tpu as pltpu

POOL_WINDOWS = (2, 4, 8, 16)
POOL_GROUPS = len(POOL_WINDOWS)
POOL_CTX = max(POOL_WINDOWS) - 1
SGU_GROUPS = 4
CHUNK = 128
CONV_WIDTH = 3
CONV_CTX = CONV_WIDTH - 1
XA_HEADS = 4
EPS = 1e-6
PAST_LEN = 16384

V7X_VMEM_LIMIT_BYTES = 60 * 1024 * 1024
SUBLANES = 8
LANES = 128
POOL_HIST = 16
CONV_HIST = 8

PROMPT_ROWS = 512
ATTN_ROWS = 1024
MEMKV_BATCH = 2
SAMPLE_TOKEN_BLOCK = 32

F32 = jnp.float32
BF16 = jnp.bfloat16


def _rms(x, g):
    return x * lax.rsqrt(jnp.mean(x * x, axis=-1, keepdims=True) + EPS) * g


def _silu(x):
    return x * jax.nn.sigmoid(x)


def _dot(a, b):
    return jnp.dot(a, b, preferred_element_type=F32)


def _layer_block(arr, layer):
    index = (layer,) + (0,) * (arr.ndim - 1)
    return pl.BlockSpec((1,) + arr.shape[1:], lambda *_: index, pipeline_mode=pl.Buffered(1))


def _params(n_grid):
    return pltpu.CompilerParams(dimension_semantics=("arbitrary",) * n_grid,
                                vmem_limit_bytes=V7X_VMEM_LIMIT_BYTES)


def _tile_row_feature(r, hd):
    return (r % XA_HEADS) * hd + (r // XA_HEADS) * LANES


def _lane_tiled(cache):
    l, b, m, h, hd = cache.shape
    t = hd // LANES
    return cache.reshape(l, b, m, h, t, LANES).transpose(0, 1, 2, 4, 3, 5).reshape(l, b, m, t * h, LANES)


def _from_lane_tiled(tiled):
    l, b, m, r, lanes = tiled.shape
    t = r // XA_HEADS
    return tiled.reshape(l, b, m, t, XA_HEADS, lanes).transpose(0, 1, 2, 4, 3, 5).reshape(
        l, b, m, XA_HEADS, t * lanes)


def _to_tile_rows(a, rows):
    hd = a.shape[1] // XA_HEADS
    pieces = [a[:, _tile_row_feature(r, hd):_tile_row_feature(r, hd) + LANES] for r in range(rows)]
    return jnp.swapaxes(jnp.stack(pieces, axis=0), 0, 1)


def _cast_specs(to_cast, n_steps, step):
    ins, outs, shapes = [], [], []
    for w, l in to_cast:
        _, k, n = w.shape
        ins.append(pl.BlockSpec((1, k // n_steps, n), lambda i, s, l=l: (l, step(i, s), 0)))
        outs.append(pl.BlockSpec((1, k // n_steps, n), lambda i, s: (0, step(i, s), 0)))
        shapes.append(jax.ShapeDtypeStruct((1, k, n), BF16))
    return ins, outs, shapes


def _cast_slabs(srcs, dsts):
    for src, dst in zip(srcs, dsts):
        dst[...] = src[...].astype(BF16)


def _memkv_kernel(*refs, n_cast):
    mem_ref, g_ref, wk_ref, wv_ref = refs[:4]
    k_ref, v_ref, kt_ref, vb_ref = refs[4 + n_cast:8 + n_cast]
    _cast_slabs(refs[4:4 + n_cast], refs[8 + n_cast:])
    nb, m, d = mem_ref.shape
    rows = k_ref.shape[3]
    mn = _rms(mem_ref[...].reshape(nb * m, d), g_ref[0]).astype(BF16)
    k = _dot(mn, wk_ref[0].astype(BF16))
    v = _dot(mn, wv_ref[0].astype(BF16))
    for i in range(nb):
        tok = slice(i * m, (i + 1) * m)
        k_ref[0, i] = _to_tile_rows(k[tok], rows)
        v_ref[0, i] = _to_tile_rows(v[tok], rows)
        kt_ref[0, i] = k[tok].T.astype(BF16)
        vb_ref[0, i] = v[tok].astype(BF16)


def _memory_kv(mem, norm_g, w_k, w_v, to_cast):
    depth, _, d = norm_g.shape
    b, m, _ = mem.shape
    rows = d // LANES
    nb = MEMKV_BATCH
    wspec = pl.BlockSpec((1, d, d), lambda l, i: (l, 0, 0))
    tiled = pl.BlockSpec((1, nb, m, rows, LANES), lambda l, i: (l, i, 0, 0, 0))
    cast_in, cast_out, cast_shapes = _cast_specs(to_cast, depth * (b // nb), lambda l, i: l * (b // nb) + i)
    outs = pl.pallas_call(
        functools.partial(_memkv_kernel, n_cast=len(to_cast)),
        grid=(depth, b // nb),
        in_specs=[pl.BlockSpec((nb, m, d), lambda l, i: (i, 0, 0)),
                  pl.BlockSpec((1, 1, d), lambda l, i: (l, 0, 0)),
                  wspec, wspec] + cast_in,
        out_specs=[tiled, tiled,
                   pl.BlockSpec((1, nb, d, m), lambda l, i: (l, i, 0, 0)),
                   pl.BlockSpec((1, nb, m, d), lambda l, i: (l, i, 0, 0))] + cast_out,
        out_shape=[jax.ShapeDtypeStruct((depth, b, m, rows, LANES), F32),
                   jax.ShapeDtypeStruct((depth, b, m, rows, LANES), F32),
                   jax.ShapeDtypeStruct((depth, b, d, m), BF16),
                   jax.ShapeDtypeStruct((depth, b, m, d), BF16)] + cast_shapes,
        compiler_params=_params(2),
        name="memory_kv",
    )(mem, norm_g, w_k, w_v, *[w for w, _ in to_cast])
    return outs[:4], outs[4:]


def _sample_attention(q_ref, k_ref, v_ref, o_ref):
    n, rows, lanes = q_ref.shape
    m = k_ref.shape[2]
    hd = rows * lanes // XA_HEADS
    for i in range(n):
        qt = q_ref[i] * (hd ** -0.5)
        mx = den = acc = None
        for t0 in range(0, m, SAMPLE_TOKEN_BLOCK):
            tok = slice(t0, t0 + SAMPLE_TOKEN_BLOCK)
            part = jnp.sum(k_ref[0, i, tok] * qt[None], axis=-1, keepdims=True)
            s = part + pltpu.roll(part, XA_HEADS, 1)
            block_max = jnp.max(s, axis=0)
            new_mx = block_max if mx is None else jnp.maximum(mx, block_max)
            e = jnp.exp(s - new_mx)
            block_den = jnp.sum(e, axis=0)
            block_acc = jnp.sum(e * v_ref[0, i, tok], axis=0)
            if mx is None:
                den, acc = block_den, block_acc
            else:
                alpha = jnp.exp(mx - new_mx)
                den = den * alpha + block_den
                acc = acc * alpha + block_acc
            mx = new_mx
        o_ref[i] = acc / den


def _sample_attention_specs(q_tiles, cache_k, layer, n_steps):
    n, rows, lanes = q_tiles.shape
    m = cache_k.shape[2]
    per_step = n // (n_steps[0] * n_steps[1])
    step = lambda i, s: i * n_steps[1] + s
    qspec = pl.BlockSpec((per_step, rows, lanes), lambda i, s: (step(i, s), 0, 0))
    kvspec = pl.BlockSpec((1, per_step, m, rows, lanes), lambda i, s: (layer, step(i, s), 0, 0, 0))
    return qspec, kvspec


def _even_prompt_kernel(x_ref, g_ref, w_in_ref, maps_ref, pscale_ref, sgw_ref, sgbias_ref, sgg_ref,
                        w_out_ref, sq_ref, sk_ref, sv_ref, xo_ref, tail_ref, so_ref,
                        xa_ref, ga_ref, u_ref, v_ref, gb_ref, y_ref, hist_ref):
    rows, w_a = y_ref.shape[0], pscale_ref.shape[-1]
    w_b = sgg_ref.shape[-1]
    gw = w_a // POOL_GROUPS
    sw = w_b // SGU_GROUPS
    s = pl.program_id(1)

    @pl.when(s == 0)
    def _():
        hist_ref[...] = jnp.zeros_like(hist_ref)

    x = x_ref[0]
    h = _rms(x, g_ref[0]).astype(BF16)
    xa_ref[...] = _dot(h, w_in_ref[0, :, 0:w_a])
    v_ref[...] = _dot(h, w_in_ref[0, :, 2 * w_a + w_b:2 * w_a + 2 * w_b])

    pos = s * rows + lax.broadcasted_iota(jnp.int32, (rows, gw), 0)
    pooled = []
    for g, w in enumerate(POOL_WINDOWS):
        cols = slice(g * gw, (g + 1) * gw)
        xa = xa_ref[:, cols]
        acc = jnp.concatenate([hist_ref[:, cols], xa], axis=0)
        shift = 1
        while shift < w:
            acc = acc + pltpu.roll(acc, shift, 0)
            shift *= 2
        cnt = jnp.minimum(pos + 1, w).astype(F32)
        pooled.append((acc[POOL_HIST:] / cnt - xa).astype(BF16))
    tail = xa_ref[rows - POOL_HIST:, :]
    hist_ref[...] = tail
    tail_ref[0] = tail

    ga_ref[...] = _dot(h, w_in_ref[0, :, w_a:2 * w_a])
    for g in range(POOL_GROUPS):
        cols = slice(g * gw, (g + 1) * gw)
        ya = _dot(pooled[g], maps_ref[0, g]) * pscale_ref[0, :, cols] * _silu(ga_ref[:, cols])
        y_ref[:, cols] = ya.astype(BF16)

    u_ref[...] = _dot(h, w_in_ref[0, :, 2 * w_a:2 * w_a + w_b])
    gb_ref[...] = _dot(h, w_in_ref[0, :, 2 * w_a + 2 * w_b:])
    out = x + _dot(y_ref[:, 0:w_a], w_out_ref[0, 0:w_a, :])

    vn = _rms(v_ref[...], sgg_ref[0]).astype(BF16)
    causal = (lax.broadcasted_iota(jnp.int32, (CHUNK, CHUNK), 0)
              >= lax.broadcasted_iota(jnp.int32, (CHUNK, CHUNK), 1))
    for g in range(SGU_GROUPS):
        cols = slice(g * sw, (g + 1) * sw)
        wg = jnp.where(causal, sgw_ref[0, g], 0.0).astype(BF16)
        for c in range(rows // CHUNK):
            rs = slice(c * CHUNK, (c + 1) * CHUNK)
            mixed = _dot(wg, vn[rs, cols]) + sgbias_ref[0, :, cols]
            y_ref[rs, w_a + g * sw:w_a + (g + 1) * sw] = (
                u_ref[rs, cols] * mixed * _silu(gb_ref[rs, cols])).astype(BF16)

    xo_ref[0] = out + _dot(y_ref[:, w_a:], w_out_ref[0, w_a:, :])
    _sample_attention(sq_ref, sk_ref, sv_ref, so_ref)


def _even_prompt(x, layer, j, norm_g, w_in, maps, pscale, sgw, sgbias, sgg, w_out, sq, cache_k, cache_v):
    b, seq, d = x.shape
    w_a, w_b = pscale.shape[-1], sgg.shape[-1]
    rows = PROMPT_ROWS
    xspec = pl.BlockSpec((1, rows, d), lambda i, s: (i, s, 0))
    qspec, kvspec = _sample_attention_specs(sq, cache_k, layer, (b, seq // rows))
    return pl.pallas_call(
        _even_prompt_kernel,
        grid=(b, seq // rows),
        in_specs=[xspec, _layer_block(norm_g, layer), _layer_block(w_in, 0), _layer_block(maps, 0),
                  _layer_block(pscale, j), _layer_block(sgw, j), _layer_block(sgbias, j),
                  _layer_block(sgg, j), _layer_block(w_out, 0), qspec, kvspec, kvspec],
        out_specs=[xspec, pl.BlockSpec((1, POOL_HIST, w_a), lambda i, s: (i, 0, 0)), qspec],
        out_shape=[jax.ShapeDtypeStruct(x.shape, F32),
                   jax.ShapeDtypeStruct((b, POOL_HIST, w_a), F32),
                   jax.ShapeDtypeStruct(sq.shape, F32)],
        scratch_shapes=[pltpu.VMEM((rows, w_a), F32), pltpu.VMEM((rows, w_a), F32),
                        pltpu.VMEM((rows, w_b), F32), pltpu.VMEM((rows, w_b), F32),
                        pltpu.VMEM((rows, w_b), F32),
                        pltpu.VMEM((rows, w_a + w_b), BF16),
                        pltpu.VMEM((POOL_HIST, w_a), F32)],
        compiler_params=_params(2),
        name="even_prompt",
    )(x, norm_g, w_in, maps, pscale, sgw, sgbias, sgg, w_out, sq, cache_k, cache_v)


def _odd_prompt_kernel(x_ref, g_ref, w_in_ref, cw_ref, w_out_ref, sq_ref, sk_ref, sv_ref,
                       xo_ref, tail_ref, so_ref, y_ref, hist_ref, *, col_slab):
    rows, w_c = y_ref.shape
    s = pl.program_id(1)

    @pl.when(s == 0)
    def _():
        hist_ref[...] = jnp.zeros_like(hist_ref)

    x = x_ref[0]
    h = _rms(x, g_ref[0]).astype(BF16)
    for c0 in range(0, w_c, col_slab):
        cols = slice(c0, c0 + col_slab)
        zb = [_dot(h, w_in_ref[0, :, part * w_c + c0:part * w_c + c0 + col_slab]) for part in range(4)]
        e = zb[1] * zb[2]
        ext = jnp.concatenate([hist_ref[:, cols], e], axis=0)
        y = (cw_ref[0, 0:1, cols] * pltpu.roll(ext, 2, 0)[CONV_HIST:]
             + cw_ref[0, 1:2, cols] * pltpu.roll(ext, 1, 0)[CONV_HIST:]
             + cw_ref[0, 2:3, cols] * e)
        tail = e[rows - CONV_HIST:]
        hist_ref[:, cols] = tail
        tail_ref[0, :, cols] = tail
        y_ref[:, cols] = (zb[0] * y * _silu(zb[3])).astype(BF16)
    xo_ref[0] = x + _dot(y_ref[...], w_out_ref[0])
    _sample_attention(sq_ref, sk_ref, sv_ref, so_ref)


def _odd_prompt(x, layer, j, norm_g, w_in, conv_w, w_out, sq, cache_k, cache_v):
    b, seq, d = x.shape
    w_c = conv_w.shape[-1]
    rows = PROMPT_ROWS
    col_slab = 512
    xspec = pl.BlockSpec((1, rows, d), lambda i, s: (i, s, 0))
    qspec, kvspec = _sample_attention_specs(sq, cache_k, layer, (b, seq // rows))
    return pl.pallas_call(
        functools.partial(_odd_prompt_kernel, col_slab=col_slab),
        grid=(b, seq // rows),
        in_specs=[xspec, _layer_block(norm_g, layer), _layer_block(w_in, 0), _layer_block(conv_w, j),
                  _layer_block(w_out, 0), qspec, kvspec, kvspec],
        out_specs=[xspec, pl.BlockSpec((1, CONV_HIST, w_c), lambda i, s: (i, 0, 0)), qspec],
        out_shape=[jax.ShapeDtypeStruct(x.shape, F32),
                   jax.ShapeDtypeStruct((b, CONV_HIST, w_c), F32),
                   jax.ShapeDtypeStruct(sq.shape, F32)],
        scratch_shapes=[pltpu.VMEM((rows, w_c), BF16),
                        pltpu.VMEM((CONV_HIST, w_c), F32)],
        compiler_params=_params(2),
        name="odd_prompt",
    )(x, norm_g, w_in, conv_w, w_out, sq, cache_k, cache_v)


def _attn_prompt_kernel(*refs, final, n_cast):
    x_ref, g_ref, wq_ref, kt_ref, v_ref, wo_ref, gf_ref = refs[:7]
    cast_src = refs[7:7 + n_cast]
    xo_ref = refs[7 + n_cast]
    cast_dst = refs[8 + n_cast:8 + 2 * n_cast]
    o_ref = refs[8 + 2 * n_cast]
    _cast_slabs(cast_src, cast_dst)
    d = x_ref.shape[-1]
    hd = d // XA_HEADS
    x = x_ref[0]
    h = _rms(x, g_ref[0]).astype(BF16)
    q = _dot(h, wq_ref[0]).astype(BF16)
    heads = [slice(hh * hd, (hh + 1) * hd) for hh in range(XA_HEADS)]
    scores = [_dot(q[:, cols], kt_ref[0, 0, cols, :]) * (hd ** -0.5) for cols in heads]
    for cols, s in zip(heads, scores):
        e = jnp.exp(s - jnp.max(s, axis=-1, keepdims=True))
        p = e / jnp.sum(e, axis=-1, keepdims=True)
        o_ref[:, cols] = _dot(p.astype(BF16), v_ref[0, 0, :, cols]).astype(BF16)
    xn = x + _dot(o_ref[...], wo_ref[0])
    if final:
        xn = _rms(xn, gf_ref[...])
    xo_ref[0] = xn


def _attn_prompt(x, layer, norm_g, w_q, kt, vb, w_o, final_g, final, to_cast):
    b, seq, d = x.shape
    m = vb.shape[2]
    rows = ATTN_ROWS
    n_steps = b * (seq // rows)
    xspec = pl.BlockSpec((1, rows, d), lambda i, s: (i, s, 0))
    cast_in, cast_out, cast_shapes = _cast_specs(to_cast, n_steps, lambda i, s: i * (seq // rows) + s)
    outs = pl.pallas_call(
        functools.partial(_attn_prompt_kernel, final=final, n_cast=len(to_cast)),
        grid=(b, seq // rows),
        in_specs=[xspec, _layer_block(norm_g, layer), _layer_block(w_q, 0),
                  pl.BlockSpec((1, 1, d, m), lambda i, s: (layer, i, 0, 0)),
                  pl.BlockSpec((1, 1, m, d), lambda i, s: (layer, i, 0, 0)),
                  _layer_block(w_o, 0),
                  pl.BlockSpec((1, d), lambda i, s: (0, 0), pipeline_mode=pl.Buffered(1))] + cast_in,
        out_specs=[xspec] + cast_out,
        out_shape=[jax.ShapeDtypeStruct(x.shape, F32)] + cast_shapes,
        scratch_shapes=[pltpu.VMEM((rows, d), BF16)],
        compiler_params=_params(2),
        name="attn_prompt",
    )(x, norm_g, w_q, kt, vb, w_o, final_g, *[w for w, _ in to_cast])
    return outs[0], outs[1:]


def _sample_enter(refs, first):
    if first:
        return refs[0][...], refs[1:]
    xs_ref, o_ref, wo_ref = refs[:3]
    return _sample_residual(xs_ref[...], o_ref, wo_ref), refs[3:]


def _sample_residual(xs, o_ref, wo_ref):
    n, d = xs.shape
    rows = o_ref.shape[0] // n
    hd = d // XA_HEADS
    acc = xs
    for r in range(rows):
        f = _tile_row_feature(r, hd)
        o_r = o_ref[pl.ds(r, n, stride=rows), :].astype(BF16)
        acc = acc + _dot(o_r, wo_ref[0, f:f + LANES, :])
    return acc


def _sample_queries(xs, g_ref, wq_ref, q_ref):
    q = _dot(_rms(xs, g_ref[0]).astype(BF16), wq_ref[0])
    q_ref[...] = _to_tile_rows(q, q_ref.shape[1])


def _carry_operand(carry):
    if carry is None:
        return [], ()
    zeros = (0,) * carry.ndim
    return [pl.BlockSpec(carry.shape, lambda i: zeros)], (carry,)


def _split_carry(refs, n_carry):
    return (refs[0], refs[1:]) if n_carry else (None, refs)


def _prev_attention_specs(o_rows, w_o_prev):
    return [pl.BlockSpec(o_rows.shape, lambda i: (0, 0)), _layer_block(w_o_prev, 0)]


def _even_sample_kernel(*refs, pos0, first, n_carry):
    xs, refs = _sample_enter(refs, first)
    (st_ref, g_ref, w_in_ref, maps_ref, pscale_ref, w00_ref, b0_ref, sgg_ref, w_out_ref, gx_ref,
     wq_ref) = refs[:11]
    carry_ref, refs = _split_carry(refs[11:], n_carry)
    xo_ref, pool_ref, vn_ref, q_ref, y_ref = refs
    w_a, w_b = pscale_ref.shape[-1], sgg_ref.shape[-1]
    gw = w_a // POOL_GROUPS
    h = _rms(xs, g_ref[0]).astype(BF16)
    xa = _dot(h, w_in_ref[0, :, 0:w_a])
    ga = _dot(h, w_in_ref[0, :, w_a:2 * w_a])
    u = _dot(h, w_in_ref[0, :, 2 * w_a:2 * w_a + w_b])
    v = _dot(h, w_in_ref[0, :, 2 * w_a + w_b:2 * w_a + 2 * w_b])
    gb = _dot(h, w_in_ref[0, :, 2 * w_a + 2 * w_b:])

    for g, w in enumerate(POOL_WINDOWS):
        cols = slice(g * gw, (g + 1) * gw)
        acc = xa[:, cols]
        for k in range(POOL_CTX - (w - 1), POOL_CTX):
            acc = acc + st_ref[0, k, :, cols]
        pooled = acc / float(min(pos0 + 1, w)) - xa[:, cols]
        ya = _dot(pooled.astype(BF16), maps_ref[0, g]) * pscale_ref[0, :, cols] * _silu(ga[:, cols])
        y_ref[:, cols] = ya.astype(BF16)
    if n_carry:
        pool_ref[0:n_carry] = carry_ref[...]
    pool_ref[n_carry, 0:POOL_CTX - 1] = st_ref[0, 1:POOL_CTX]
    pool_ref[n_carry, POOL_CTX - 1] = xa

    vn = _rms(v, sgg_ref[0])
    vn_ref[0] = vn
    mixed = w00_ref[0] * vn + b0_ref[0]
    y_ref[:, w_a:] = (u * mixed * _silu(gb)).astype(BF16)
    xn = xs + _dot(y_ref[...], w_out_ref[0])
    xo_ref[...] = xn
    _sample_queries(xn, gx_ref, wq_ref, q_ref)


def _even_sample(xs, prev, layer, j, state, carry, norm_g, w_in, maps, pscale, w00, b0, sgg, w_out,
                 norm_xg, w_q, w_o, pos0):
    n, d = xs.shape
    _, ctx, _, w_a = state.shape
    w_b = sgg.shape[-1]
    whole = pl.BlockSpec((n, d), lambda i: (0, 0))
    qshape = (n, d // LANES, LANES)
    first = prev is None
    head_specs = [whole] + ([] if first else _prev_attention_specs(prev, w_o))
    head_args = (xs,) if first else (xs, prev, w_o)
    carry_specs, carry_args = _carry_operand(carry)
    return pl.pallas_call(
        functools.partial(_even_sample_kernel, pos0=pos0, first=first, n_carry=j),
        grid=(1,),
        in_specs=head_specs + [
            _layer_block(state, j), _layer_block(norm_g, layer), _layer_block(w_in, 0),
            _layer_block(maps, 0), _layer_block(pscale, j), _layer_block(w00, j), _layer_block(b0, j),
            _layer_block(sgg, j), _layer_block(w_out, 0), _layer_block(norm_xg, layer),
            _layer_block(w_q, 0)] + carry_specs,
        out_specs=[whole, pl.BlockSpec((j + 1, ctx, n, w_a), lambda i: (0, 0, 0, 0)),
                   pl.BlockSpec((1, n, w_b), lambda i: (0, 0, 0)),
                   pl.BlockSpec(qshape, lambda i: (0, 0, 0))],
        out_shape=[jax.ShapeDtypeStruct((n, d), F32),
                   jax.ShapeDtypeStruct((j + 1, ctx, n, w_a), F32),
                   jax.ShapeDtypeStruct((1, n, w_b), F32),
                   jax.ShapeDtypeStruct(qshape, F32)],
        scratch_shapes=[pltpu.VMEM((n, w_a + w_b), BF16)],
        compiler_params=_params(1),
        name="even_sample",
    )(*head_args, state, norm_g, w_in, maps, pscale, w00, b0, sgg, w_out, norm_xg, w_q, *carry_args)


def _odd_sample_kernel(*refs, first, n_carry):
    xs, refs = _sample_enter(refs, first)
    st_ref, g_ref, w_in_ref, cw_ref, w_out_ref, gx_ref, wq_ref = refs[:7]
    carry_ref, refs = _split_carry(refs[7:], n_carry)
    xo_ref, conv_ref, q_ref, y_ref = refs
    w_c = cw_ref.shape[-1]
    h = _rms(xs, g_ref[0]).astype(BF16)
    bg = _dot(h, w_in_ref[0, :, 0:w_c])
    cg = _dot(h, w_in_ref[0, :, w_c:2 * w_c])
    xc = _dot(h, w_in_ref[0, :, 2 * w_c:3 * w_c])
    gate = _dot(h, w_in_ref[0, :, 3 * w_c:])
    e = cg * xc
    newest = st_ref[0, :, CONV_CTX - 1, :]
    y = cw_ref[0, 0:1, :] * st_ref[0, :, 0, :] + cw_ref[0, 1:2, :] * newest + cw_ref[0, 2:3, :] * e
    if n_carry:
        conv_ref[0:n_carry] = carry_ref[...]
    conv_ref[n_carry, :, 0, :] = newest
    conv_ref[n_carry, :, CONV_CTX - 1, :] = e
    y_ref[...] = (bg * y * _silu(gate)).astype(BF16)
    xn = xs + _dot(y_ref[...], w_out_ref[0])
    xo_ref[...] = xn
    _sample_queries(xn, gx_ref, wq_ref, q_ref)


def _odd_sample(xs, prev, layer, j, state, carry, norm_g, w_in, conv_w, w_out, norm_xg, w_q, w_o):
    n, d = xs.shape
    w_c = conv_w.shape[-1]
    whole = pl.BlockSpec((n, d), lambda i: (0, 0))
    stspec = pl.BlockSpec((1, n, CONV_CTX, w_c), lambda i: (j, 0, 0, 0))
    qshape = (n, d // LANES, LANES)
    first = prev is None
    head_specs = [whole] + ([] if first else _prev_attention_specs(prev, w_o))
    head_args = (xs,) if first else (xs, prev, w_o)
    carry_specs, carry_args = _carry_operand(carry)
    return pl.pallas_call(
        functools.partial(_odd_sample_kernel, first=first, n_carry=j),
        grid=(1,),
        in_specs=head_specs + [
            stspec, _layer_block(norm_g, layer), _layer_block(w_in, 0), _layer_block(conv_w, j),
            _layer_block(w_out, 0), _layer_block(norm_xg, layer), _layer_block(w_q, 0)] + carry_specs,
        out_specs=[whole, pl.BlockSpec((j + 1, n, CONV_CTX, w_c), lambda i: (0, 0, 0, 0)),
                   pl.BlockSpec(qshape, lambda i: (0, 0, 0))],
        out_shape=[jax.ShapeDtypeStruct((n, d), F32),
                   jax.ShapeDtypeStruct((j + 1, n, CONV_CTX, w_c), F32),
                   jax.ShapeDtypeStruct(qshape, F32)],
        scratch_shapes=[pltpu.VMEM((n, w_c), BF16)],
        compiler_params=_params(1),
        name="odd_sample",
    )(*head_args, state, norm_g, w_in, conv_w, w_out, norm_xg, w_q, *carry_args)


def _sample_finish_kernel(xs_ref, o_ref, wo_ref, gf_ref, y_ref):
    y_ref[...] = _rms(_sample_residual(xs_ref[...], o_ref, wo_ref), gf_ref[...])


def _sample_finish(xs, prev, w_o, final_g):
    n, d = xs.shape
    whole = pl.BlockSpec((n, d), lambda i: (0, 0))
    return pl.pallas_call(
        _sample_finish_kernel,
        grid=(1,),
        in_specs=[whole] + _prev_attention_specs(prev, w_o)
                 + [pl.BlockSpec((1, d), lambda i: (0, 0))],
        out_specs=whole,
        out_shape=jax.ShapeDtypeStruct((n, d), F32),
        compiler_params=_params(1),
        name="sample_finish",
    )(xs, prev, w_o, final_g)


def kernel(x_prompt, x_sample, mem_prompt, state_pool, state_conv, cache_mem_k, cache_mem_v, norm_mix_g, norm_xattn_g, norm_mem_g, w_in_ab, pool_maps, pool_scale, sgu_w, sgu_b, sgu_g, w_out_ab, w_in_c, conv_w, w_out_c, w_q, w_k, w_v, w_o, norm_final_g):
    depth, d = norm_mix_g.shape
    n_s, dec_seq, _ = x_sample.shape
    n_even, n_odd = pool_scale.shape[0], conv_w.shape[0]
    w_a, w_b, w_c = pool_scale.shape[-1], sgu_g.shape[-1], conv_w.shape[-1]
    hd = d // XA_HEADS
    assert dec_seq == 1, "sample group is one new token per sequence"
    assert PAST_LEN % CHUNK == 0, "the sample token must open a spatial-gating chunk"
    assert (hd // LANES) * XA_HEADS == SUBLANES, "one memory token per (8, 128) register tile"

    maps_rows = pool_maps.reshape(n_even, w_a, w_a // POOL_GROUPS)

    def matmul_weights(i):
        j = i // 2
        mixer = [(w_in_ab, j), (w_out_ab, j), (maps_rows, j)] if i % 2 == 0 else [(w_in_c, j), (w_out_c, j)]
        return mixer + [(w_q, i), (w_o, i)]

    row = lambda a: a.reshape(a.shape[0], 1, a.shape[-1])
    g_mix, g_xattn, g_mem = row(norm_mix_g), row(norm_xattn_g), row(norm_mem_g)
    pscale, sgg = row(pool_scale), row(sgu_g)
    g_final = norm_final_g.reshape(1, d)
    sw = w_b // SGU_GROUPS
    sgbias = jnp.repeat(jnp.swapaxes(sgu_b, 1, 2), sw, axis=2)
    w00 = row(jnp.repeat(sgu_w[:, :, 0, 0], sw, axis=1))
    b0 = row(jnp.repeat(sgu_b[:, :, 0], sw, axis=1))

    (k_p, v_p, kt_p, vb_p), wts = _memory_kv(mem_prompt, g_mem, w_k, w_v, matmul_weights(0))
    cache_k = _lane_tiled(cache_mem_k)
    cache_v = _lane_tiled(cache_mem_v)
    pool_state = jnp.swapaxes(state_pool, 1, 2)

    xp = x_prompt
    xs = x_sample.reshape(n_s, d)
    so = wo_prev = None
    pool_s = conv_s = None
    pool_p, conv_p, vrows_s = [], [], []
    for i in range(depth):
        j = i // 2
        if i % 2 == 0:
            w_in_b, w_out_b, maps_b, wq_b, wo_b = wts
            maps_b = maps_b.reshape(1, POOL_GROUPS, w_a // POOL_GROUPS, w_a // POOL_GROUPS)
            xs, pool_s, vn, sq = _even_sample(xs, so, i, j, pool_state, pool_s, g_mix, w_in_b, maps_b,
                                                pscale, w00, b0, sgg, w_out_b, g_xattn, wq_b,
                                                wo_prev, PAST_LEN)
            vrows_s.append(vn)
            xp, tail, so = _even_prompt(xp, i, j, g_mix, w_in_b, maps_b, pscale, sgu_w, sgbias, sgg,
                                        w_out_b, sq, cache_k, cache_v)
            pool_p.append(tail[:, POOL_HIST - POOL_CTX:])
        else:
            w_in_b, w_out_b, wq_b, wo_b = wts
            xs, conv_s, sq = _odd_sample(xs, so, i, j, state_conv, conv_s, g_mix, w_in_b, conv_w,
                                           w_out_b, g_xattn, wq_b, wo_prev)
            xp, tail, so = _odd_prompt(xp, i, j, g_mix, w_in_b, conv_w, w_out_b, sq, cache_k, cache_v)
            conv_p.append(tail[:, CONV_HIST - CONV_CTX:])
        so = so.reshape(n_s * so.shape[1], LANES)
        wo_prev = wo_b
        xp, wts = _attn_prompt(xp, i, g_xattn, wq_b, kt_p, vb_p, wo_b, g_final, i == depth - 1,
                               matmul_weights(i + 1) if i + 1 < depth else [])
    ys = _sample_finish(xs, so, wo_prev, g_final)

    new_pool_s = jnp.swapaxes(pool_s, 1, 2)
    new_vrows = jnp.concatenate(vrows_s, axis=0).reshape(n_even, n_s, 1, w_b)
    return (xp, ys.reshape(n_s, 1, d), jnp.stack(pool_p), new_pool_s, jnp.stack(conv_p),
            conv_s, new_vrows, _from_lane_tiled(k_p), _from_lane_tiled(v_p))
```

```python
import functools

import jax
import jax.numpy as jnp
from jax import lax
from jax.experimental import pallas as pl
from jax.experimental.pallas import tpu as pltpu

POOL_WINDOWS = (2, 4, 8, 16)
POOL_GROUPS = len(POOL_WINDOWS)
POOL_CTX = max(POOL_WINDOWS) - 1
SGU_GROUPS = 4
CHUNK = 128
CONV_WIDTH = 3
CONV_CTX = CONV_WIDTH - 1
XA_HEADS = 4
EPS = 1e-6
PAST_LEN = 16384

V7X_VMEM_LIMIT_BYTES = 60 * 1024 * 1024
SUBLANES = 8
LANES = 128
POOL_HIST = 16
CONV_HIST = 8

PROMPT_ROWS = 512
CONV_COL_SLAB = 512
ATTN_ROWS = 1024
MEMKV_BATCH = 2
SAMPLE_TOKEN_BLOCK = 64

F32 = jnp.float32
BF16 = jnp.bfloat16


def _rms(x, g):
    return x * lax.rsqrt(jnp.mean(x * x, axis=-1, keepdims=True) + EPS) * g


def _silu(x):
    return x * jax.nn.sigmoid(x)


def _dot(a, b):
    return jnp.dot(a, b, preferred_element_type=F32)


def _layer_block(arr, layer):
    index = (layer,) + (0,) * (arr.ndim - 1)
    return pl.BlockSpec((1,) + arr.shape[1:], lambda *_: index, pipeline_mode=pl.Buffered(1))


def _params(n_grid):
    return pltpu.CompilerParams(dimension_semantics=("arbitrary",) * n_grid,
                                vmem_limit_bytes=V7X_VMEM_LIMIT_BYTES)


def _tile_row_feature(r, hd):
    return (r % XA_HEADS) * hd + (r // XA_HEADS) * LANES


def _lane_tiled(cache):
    l, b, m, h, hd = cache.shape
    t = hd // LANES
    return cache.reshape(l, b, m, h, t, LANES).transpose(0, 1, 2, 4, 3, 5).reshape(l, b, m, t * h, LANES)


def _from_lane_tiled(tiled):
    l, b, m, r, lanes = tiled.shape
    t = r // XA_HEADS
    return tiled.reshape(l, b, m, t, XA_HEADS, lanes).transpose(0, 1, 2, 4, 3, 5).reshape(
        l, b, m, XA_HEADS, t * lanes)


def _to_tile_rows(a, rows):
    hd = a.shape[1] // XA_HEADS
    pieces = [a[:, _tile_row_feature(r, hd):_tile_row_feature(r, hd) + LANES] for r in range(rows)]
    return jnp.swapaxes(jnp.stack(pieces, axis=0), 0, 1)


def _cast_specs(to_cast, n_steps, step):
    ins, outs, shapes = [], [], []
    for w, l in to_cast:
        _, k, n = w.shape
        ins.append(pl.BlockSpec((1, k // n_steps, n), lambda i, s, l=l: (l, step(i, s), 0)))
        outs.append(pl.BlockSpec((1, k // n_steps, n), lambda i, s: (0, step(i, s), 0)))
        shapes.append(jax.ShapeDtypeStruct((1, k, n), BF16))
    return ins, outs, shapes


def _cast_slabs(srcs, dsts):
    for src, dst in zip(srcs, dsts):
        dst[...] = src[...].astype(BF16)


def _memkv_kernel(*refs, n_cast):
    mem_ref, g_ref, wk_ref, wv_ref = refs[:4]
    k_ref, v_ref, kt_ref, vb_ref = refs[4 + n_cast:8 + n_cast]
    _cast_slabs(refs[4:4 + n_cast], refs[8 + n_cast:])
    nb, m, d = mem_ref.shape
    rows = k_ref.shape[3]
    mn = _rms(mem_ref[...].reshape(nb * m, d), g_ref[0]).astype(BF16)
    k = _dot(mn, wk_ref[0].astype(BF16))
    v = _dot(mn, wv_ref[0].astype(BF16))
    for i in range(nb):
        tok = slice(i * m, (i + 1) * m)
        k_ref[0, i] = _to_tile_rows(k[tok], rows)
        v_ref[0, i] = _to_tile_rows(v[tok], rows)
        kt_ref[0, i] = k[tok].T.astype(BF16)
        vb_ref[0, i] = v[tok].astype(BF16)


def _memory_kv(mem, norm_g, w_k, w_v, to_cast):
    depth, _, d = norm_g.shape
    b, m, _ = mem.shape
    rows = d // LANES
    nb = MEMKV_BATCH
    wspec = pl.BlockSpec((1, d, d), lambda l, i: (l, 0, 0))
    tiled = pl.BlockSpec((1, nb, m, rows, LANES), lambda l, i: (l, i, 0, 0, 0))
    cast_in, cast_out, cast_shapes = _cast_specs(to_cast, depth * (b // nb), lambda l, i: l * (b // nb) + i)
    outs = pl.pallas_call(
        functools.partial(_memkv_kernel, n_cast=len(to_cast)),
        grid=(depth, b // nb),
        in_specs=[pl.BlockSpec((nb, m, d), lambda l, i: (i, 0, 0)),
                  pl.BlockSpec((1, 1, d), lambda l, i: (l, 0, 0)),
                  wspec, wspec] + cast_in,
        out_specs=[tiled, tiled,
                   pl.BlockSpec((1, nb, d, m), lambda l, i: (l, i, 0, 0)),
                   pl.BlockSpec((1, nb, m, d), lambda l, i: (l, i, 0, 0))] + cast_out,
        out_shape=[jax.ShapeDtypeStruct((depth, b, m, rows, LANES), F32),
                   jax.ShapeDtypeStruct((depth, b, m, rows, LANES), F32),
                   jax.ShapeDtypeStruct((depth, b, d, m), BF16),
                   jax.ShapeDtypeStruct((depth, b, m, d), BF16)] + cast_shapes,
        compiler_params=_params(2),
        name="memory_kv",
    )(mem, norm_g, w_k, w_v, *[w for w, _ in to_cast])
    return outs[:4], outs[4:]


def _sample_attention(q_ref, k_ref, v_ref, o_ref):
    n, rows, lanes = q_ref.shape
    m = k_ref.shape[2]
    hd = rows * lanes // XA_HEADS
    for i in range(n):
        qt = q_ref[i] * (hd ** -0.5)
        mx = den = acc = None
        for t0 in range(0, m, SAMPLE_TOKEN_BLOCK):
            tok = slice(t0, t0 + SAMPLE_TOKEN_BLOCK)
            part = jnp.sum(k_ref[0, i, tok] * qt[None], axis=-1, keepdims=True)
            s = part + pltpu.roll(part, XA_HEADS, 1)
            block_max = jnp.max(s, axis=0)
            new_mx = block_max if mx is None else jnp.maximum(mx, block_max)
            e = jnp.exp(s - new_mx)
            block_den = jnp.sum(e, axis=0)
            block_acc = jnp.sum(e * v_ref[0, i, tok], axis=0)
            if mx is None:
                den, acc = block_den, block_acc
            else:
                alpha = jnp.exp(mx - new_mx)
                den = den * alpha + block_den
                acc = acc * alpha + block_acc
            mx = new_mx
        o_ref[i] = acc / den


def _sample_attention_specs(q_tiles, cache_k, layer, n_steps):
    n, rows, lanes = q_tiles.shape
    m = cache_k.shape[2]
    per_step = n // (n_steps[0] * n_steps[1])
    step = lambda i, s: i * n_steps[1] + s
    qspec = pl.BlockSpec((per_step, rows, lanes), lambda i, s: (step(i, s), 0, 0))
    kvspec = pl.BlockSpec((1, per_step, m, rows, lanes), lambda i, s: (layer, step(i, s), 0, 0, 0))
    return qspec, kvspec


def _even_prompt_kernel(x_ref, g_ref, w_in_ref, maps_ref, pscale_ref, sgw_ref, sgbias_ref, sgg_ref,
                        w_out_ref, sq_ref, sk_ref, sv_ref, xo_ref, tail_ref, so_ref,
                        xa_ref, ga_ref, u_ref, v_ref, gb_ref, y_ref, hist_ref):
    rows, w_a = y_ref.shape[0], pscale_ref.shape[-1]
    w_b = sgg_ref.shape[-1]
    gw = w_a // POOL_GROUPS
    sw = w_b // SGU_GROUPS
    s = pl.program_id(1)

    @pl.when(s == 0)
    def _():
        hist_ref[...] = jnp.zeros_like(hist_ref)

    x = x_ref[0]
    h = _rms(x, g_ref[0]).astype(BF16)
    xa_ref[...] = _dot(h, w_in_ref[0, :, 0:w_a])
    v_ref[...] = _dot(h, w_in_ref[0, :, 2 * w_a + w_b:2 * w_a + 2 * w_b])

    pos = s * rows + lax.broadcasted_iota(jnp.int32, (rows, gw), 0)
    pooled = []
    for g, w in enumerate(POOL_WINDOWS):
        cols = slice(g * gw, (g + 1) * gw)
        xa = xa_ref[:, cols]
        acc = jnp.concatenate([hist_ref[:, cols], xa], axis=0)
        shift = 1
        while shift < w:
            acc = acc + pltpu.roll(acc, shift, 0)
            shift *= 2
        cnt = jnp.minimum(pos + 1, w).astype(F32)
        pooled.append((acc[POOL_HIST:] / cnt - xa).astype(BF16))
    tail = xa_ref[rows - POOL_HIST:, :]
    hist_ref[...] = tail
    tail_ref[0] = tail

    ga_ref[...] = _dot(h, w_in_ref[0, :, w_a:2 * w_a])
    for g in range(POOL_GROUPS):
        cols = slice(g * gw, (g + 1) * gw)
        ya = _dot(pooled[g], maps_ref[0, g]) * pscale_ref[0, :, cols] * _silu(ga_ref[:, cols])
        y_ref[:, cols] = ya.astype(BF16)

    u_ref[...] = _dot(h, w_in_ref[0, :, 2 * w_a:2 * w_a + w_b])
    gb_ref[...] = _dot(h, w_in_ref[0, :, 2 * w_a + 2 * w_b:])
    out = x + _dot(y_ref[:, 0:w_a], w_out_ref[0, 0:w_a, :])

    vn = _rms(v_ref[...], sgg_ref[0]).astype(BF16)
    causal = (lax.broadcasted_iota(jnp.int32, (CHUNK, CHUNK), 0)
              >= lax.broadcasted_iota(jnp.int32, (CHUNK, CHUNK), 1))
    for g in range(SGU_GROUPS):
        cols = slice(g * sw, (g + 1) * sw)
        wg = jnp.where(causal, sgw_ref[0, g], 0.0).astype(BF16)
        for c in range(rows // CHUNK):
            rs = slice(c * CHUNK, (c + 1) * CHUNK)
            mixed = _dot(wg, vn[rs, cols]) + sgbias_ref[0, :, cols]
            y_ref[rs, w_a + g * sw:w_a + (g + 1) * sw] = (
                u_ref[rs, cols] * mixed * _silu(gb_ref[rs, cols])).astype(BF16)

    xo_ref[0] = out + _dot(y_ref[:, w_a:], w_out_ref[0, w_a:, :])
    _sample_attention(sq_ref, sk_ref, sv_ref, so_ref)


def _even_prompt(x, layer, j, norm_g, w_in, maps, pscale, sgw, sgbias, sgg, w_out, sq, cache_k, cache_v):
    b, seq, d = x.shape
    w_a, w_b = pscale.shape[-1], sgg.shape[-1]
    rows = PROMPT_ROWS
    xspec = pl.BlockSpec((1, rows, d), lambda i, s: (i, s, 0))
    qspec, kvspec = _sample_attention_specs(sq, cache_k, layer, (b, seq // rows))
    return pl.pallas_call(
        _even_prompt_kernel,
        grid=(b, seq // rows),
        in_specs=[xspec, _layer_block(norm_g, layer), _layer_block(w_in, 0), _layer_block(maps, 0),
                  _layer_block(pscale, j), _layer_block(sgw, j), _layer_block(sgbias, j),
                  _layer_block(sgg, j), _layer_block(w_out, 0), qspec, kvspec, kvspec],
        out_specs=[xspec, pl.BlockSpec((1, POOL_HIST, w_a), lambda i, s: (i, 0, 0)), qspec],
        out_shape=[jax.ShapeDtypeStruct(x.shape, F32),
                   jax.ShapeDtypeStruct((b, POOL_HIST, w_a), F32),
                   jax.ShapeDtypeStruct(sq.shape, F32)],
        scratch_shapes=[pltpu.VMEM((rows, w_a), F32), pltpu.VMEM((rows, w_a), F32),
                        pltpu.VMEM((rows, w_b), F32), pltpu.VMEM((rows, w_b), F32),
                        pltpu.VMEM((rows, w_b), F32),
                        pltpu.VMEM((rows, w_a + w_b), BF16),
                        pltpu.VMEM((POOL_HIST, w_a), F32)],
        compiler_params=_params(2),
        name="even_prompt",
    )(x, norm_g, w_in, maps, pscale, sgw, sgbias, sgg, w_out, sq, cache_k, cache_v)


def _odd_prompt_kernel(x_ref, g_ref, w_in_ref, cw_ref, w_out_ref, sq_ref, sk_ref, sv_ref,
                       xo_ref, tail_ref, so_ref, y_ref, hist_ref, *, col_slab):
    rows, w_c = y_ref.shape
    s = pl.program_id(1)

    @pl.when(s == 0)
    def _():
        hist_ref[...] = jnp.zeros_like(hist_ref)

    x = x_ref[0]
    h = _rms(x, g_ref[0]).astype(BF16)
    for c0 in range(0, w_c, col_slab):
        cols = slice(c0, c0 + col_slab)
        zb = [_dot(h, w_in_ref[0, :, part * w_c + c0:part * w_c + c0 + col_slab]) for part in range(4)]
        e = zb[1] * zb[2]
        ext = jnp.concatenate([hist_ref[:, cols], e], axis=0)
        y = (cw_ref[0, 0:1, cols] * pltpu.roll(ext, 2, 0)[CONV_HIST:]
             + cw_ref[0, 1:2, cols] * pltpu.roll(ext, 1, 0)[CONV_HIST:]
             + cw_ref[0, 2:3, cols] * e)
        tail = e[rows - CONV_HIST:]
        hist_ref[:, cols] = tail
        tail_ref[0, :, cols] = tail
        y_ref[:, cols] = (zb[0] * y * _silu(zb[3])).astype(BF16)
    xo_ref[0] = x + _dot(y_ref[...], w_out_ref[0])
    _sample_attention(sq_ref, sk_ref, sv_ref, so_ref)


def _odd_prompt(x, layer, j, norm_g, w_in, conv_w, w_out, sq, cache_k, cache_v):
    b, seq, d = x.shape
    w_c = conv_w.shape[-1]
    rows = PROMPT_ROWS
    xspec = pl.BlockSpec((1, rows, d), lambda i, s: (i, s, 0))
    qspec, kvspec = _sample_attention_specs(sq, cache_k, layer, (b, seq // rows))
    return pl.pallas_call(
        functools.partial(_odd_prompt_kernel, col_slab=CONV_COL_SLAB),
        grid=(b, seq // rows),
        in_specs=[xspec, _layer_block(norm_g, layer), _layer_block(w_in, 0), _layer_block(conv_w, j),
                  _layer_block(w_out, 0), qspec, kvspec, kvspec],
        out_specs=[xspec, pl.BlockSpec((1, CONV_HIST, w_c), lambda i, s: (i, 0, 0)), qspec],
        out_shape=[jax.ShapeDtypeStruct(x.shape, F32),
                   jax.ShapeDtypeStruct((b, CONV_HIST, w_c), F32),
                   jax.ShapeDtypeStruct(sq.shape, F32)],
        scratch_shapes=[pltpu.VMEM((rows, w_c), BF16),
                        pltpu.VMEM((CONV_HIST, w_c), F32)],
        compiler_params=_params(2),
        name="odd_prompt",
    )(x, norm_g, w_in, conv_w, w_out, sq, cache_k, cache_v)


def _attn_prompt_kernel(*refs, final, n_cast):
    x_ref, g_ref, wq_ref, kt_ref, v_ref, wo_ref, gf_ref = refs[:7]
    cast_src = refs[7:7 + n_cast]
    xo_ref = refs[7 + n_cast]
    cast_dst = refs[8 + n_cast:8 + 2 * n_cast]
    o_ref = refs[8 + 2 * n_cast]
    _cast_slabs(cast_src, cast_dst)
    d = x_ref.shape[-1]
    hd = d // XA_HEADS
    x = x_ref[0]
    h = _rms(x, g_ref[0]).astype(BF16)
    q = _dot(h, wq_ref[0]).astype(BF16)
    heads = [slice(hh * hd, (hh + 1) * hd) for hh in range(XA_HEADS)]
    scores = [_dot(q[:, cols], kt_ref[0, 0, cols, :]) * (hd ** -0.5) for cols in heads]
    for cols, s in zip(heads, scores):
        e = jnp.exp(s - jnp.max(s, axis=-1, keepdims=True))
        p = e / jnp.sum(e, axis=-1, keepdims=True)
        o_ref[:, cols] = _dot(p.astype(BF16), v_ref[0, 0, :, cols]).astype(BF16)
    xn = x + _dot(o_ref[...], wo_ref[0])
    if final:
        xn = _rms(xn, gf_ref[...])
    xo_ref[0] = xn


def _attn_prompt(x, layer, norm_g, w_q, kt, vb, w_o, final_g, final, to_cast):
    b, seq, d = x.shape
    m = vb.shape[2]
    rows = ATTN_ROWS
    n_steps = b * (seq // rows)
    xspec = pl.BlockSpec((1, rows, d), lambda i, s: (i, s, 0))
    cast_in, cast_out, cast_shapes = _cast_specs(to_cast, n_steps, lambda i, s: i * (seq // rows) + s)
    outs = pl.pallas_call(
        functools.partial(_attn_prompt_kernel, final=final, n_cast=len(to_cast)),
        grid=(b, seq // rows),
        in_specs=[xspec, _layer_block(norm_g, layer), _layer_block(w_q, 0),
                  pl.BlockSpec((1, 1, d, m), lambda i, s: (layer, i, 0, 0)),
                  pl.BlockSpec((1, 1, m, d), lambda i, s: (layer, i, 0, 0)),
                  _layer_block(w_o, 0),
                  pl.BlockSpec((1, d), lambda i, s: (0, 0), pipeline_mode=pl.Buffered(1))] + cast_in,
        out_specs=[xspec] + cast_out,
        out_shape=[jax.ShapeDtypeStruct(x.shape, F32)] + cast_shapes,
        scratch_shapes=[pltpu.VMEM((rows, d), BF16)],
        compiler_params=_params(2),
        name="attn_prompt",
    )(x, norm_g, w_q, kt, vb, w_o, final_g, *[w for w, _ in to_cast])
    return outs[0], outs[1:]


def _sample_enter(refs, first):
    if first:
        return refs[0][...], refs[1:]
    xs_ref, o_ref, wo_ref = refs[:3]
    return _sample_residual(xs_ref[...], o_ref, wo_ref), refs[3:]


def _sample_residual(xs, o_ref, wo_ref):
    n, d = xs.shape
    rows = o_ref.shape[0] // n
    hd = d // XA_HEADS
    acc = xs
    for r in range(rows):
        f = _tile_row_feature(r, hd)
        o_r = o_ref[pl.ds(r, n, stride=rows), :].astype(BF16)
        acc = acc + _dot(o_r, wo_ref[0, f:f + LANES, :])
    return acc


def _sample_queries(xs, g_ref, wq_ref, q_ref):
    q = _dot(_rms(xs, g_ref[0]).astype(BF16), wq_ref[0])
    q_ref[...] = _to_tile_rows(q, q_ref.shape[1])


def _carry_operand(carry):
    if carry is None:
        return [], ()
    zeros = (0,) * carry.ndim
    return [pl.BlockSpec(carry.shape, lambda i: zeros)], (carry,)


def _split_carry(refs, n_carry):
    return (refs[0], refs[1:]) if n_carry else (None, refs)


def _prev_attention_specs(o_rows, w_o_prev):
    return [pl.BlockSpec(o_rows.shape, lambda i: (0, 0)), _layer_block(w_o_prev, 0)]


def _even_sample_kernel(*refs, pos0, first, n_carry):
    xs, refs = _sample_enter(refs, first)
    (st_ref, g_ref, w_in_ref, maps_ref, pscale_ref, w00_ref, b0_ref, sgg_ref, w_out_ref, gx_ref,
     wq_ref) = refs[:11]
    carry_ref, refs = _split_carry(refs[11:], n_carry)
    xo_ref, pool_ref, vn_ref, q_ref, y_ref = refs
    w_a, w_b = pscale_ref.shape[-1], sgg_ref.shape[-1]
    gw = w_a // POOL_GROUPS
    h = _rms(xs, g_ref[0]).astype(BF16)
    xa = _dot(h, w_in_ref[0, :, 0:w_a])
    ga = _dot(h, w_in_ref[0, :, w_a:2 * w_a])
    u = _dot(h, w_in_ref[0, :, 2 * w_a:2 * w_a + w_b])
    v = _dot(h, w_in_ref[0, :, 2 * w_a + w_b:2 * w_a + 2 * w_b])
    gb = _dot(h, w_in_ref[0, :, 2 * w_a + 2 * w_b:])

    for g, w in enumerate(POOL_WINDOWS):
        cols = slice(g * gw, (g + 1) * gw)
        acc = xa[:, cols]
        for k in range(POOL_CTX - (w - 1), POOL_CTX):
            acc = acc + st_ref[0, k, :, cols]
        pooled = acc / float(min(pos0 + 1, w)) - xa[:, cols]
        ya = _dot(pooled.astype(BF16), maps_ref[0, g]) * pscale_ref[0, :, cols] * _silu(ga[:, cols])
        y_ref[:, cols] = ya.astype(BF16)
    if n_carry:
        pool_ref[0:n_carry] = carry_ref[...]
    pool_ref[n_carry, 0:POOL_CTX - 1] = st_ref[0, 1:POOL_CTX]
    pool_ref[n_carry, POOL_CTX - 1] = xa

    vn = _rms(v, sgg_ref[0])
    vn_ref[0] = vn
    mixed = w00_ref[0] * vn + b0_ref[0]
    y_ref[:, w_a:] = (u * mixed * _silu(gb)).astype(BF16)
    xn = xs + _dot(y_ref[...], w_out_ref[0])
    xo_ref[...] = xn
    _sample_queries(xn, gx_ref, wq_ref, q_ref)


def _even_sample(xs, prev, layer, j, state, carry, norm_g, w_in, maps, pscale, w00, b0, sgg, w_out,
                 norm_xg, w_q, w_o, pos0):
    n, d = xs.shape
    _, ctx, _, w_a = state.shape
    w_b = sgg.shape[-1]
    whole = pl.BlockSpec((n, d), lambda i: (0, 0))
    qshape = (n, d // LANES, LANES)
    first = prev is None
    head_specs = [whole] + ([] if first else _prev_attention_specs(prev, w_o))
    head_args = (xs,) if first else (xs, prev, w_o)
    carry_specs, carry_args = _carry_operand(carry)
    return pl.pallas_call(
        functools.partial(_even_sample_kernel, pos0=pos0, first=first, n_carry=j),
        grid=(1,),
        in_specs=head_specs + [
            _layer_block(state, j), _layer_block(norm_g, layer), _layer_block(w_in, 0),
            _layer_block(maps, 0), _layer_block(pscale, j), _layer_block(w00, j), _layer_block(b0, j),
            _layer_block(sgg, j), _layer_block(w_out, 0), _layer_block(norm_xg, layer),
            _layer_block(w_q, 0)] + carry_specs,
        out_specs=[whole, pl.BlockSpec((j + 1, ctx, n, w_a), lambda i: (0, 0, 0, 0)),
                   pl.BlockSpec((1, n, w_b), lambda i: (0, 0, 0)),
                   pl.BlockSpec(qshape, lambda i: (0, 0, 0))],
        out_shape=[jax.ShapeDtypeStruct((n, d), F32),
                   jax.ShapeDtypeStruct((j + 1, ctx, n, w_a), F32),
                   jax.ShapeDtypeStruct((1, n, w_b), F32),
                   jax.ShapeDtypeStruct(qshape, F32)],
        scratch_shapes=[pltpu.VMEM((n, w_a + w_b), BF16)],
        compiler_params=_params(1),
        name="even_sample",
    )(*head_args, state, norm_g, w_in, maps, pscale, w00, b0, sgg, w_out, norm_xg, w_q, *carry_args)


def _odd_sample_kernel(*refs, first, n_carry):
    xs, refs = _sample_enter(refs, first)
    st_ref, g_ref, w_in_ref, cw_ref, w_out_ref, gx_ref, wq_ref = refs[:7]
    carry_ref, refs = _split_carry(refs[7:], n_carry)
    xo_ref, conv_ref, q_ref, y_ref = refs
    w_c = cw_ref.shape[-1]
    h = _rms(xs, g_ref[0]).astype(BF16)
    bg = _dot(h, w_in_ref[0, :, 0:w_c])
    cg = _dot(h, w_in_ref[0, :, w_c:2 * w_c])
    xc = _dot(h, w_in_ref[0, :, 2 * w_c:3 * w_c])
    gate = _dot(h, w_in_ref[0, :, 3 * w_c:])
    e = cg * xc
    newest = st_ref[0, :, CONV_CTX - 1, :]
    y = cw_ref[0, 0:1, :] * st_ref[0, :, 0, :] + cw_ref[0, 1:2, :] * newest + cw_ref[0, 2:3, :] * e
    if n_carry:
        conv_ref[0:n_carry] = carry_ref[...]
    conv_ref[n_carry, :, 0, :] = newest
    conv_ref[n_carry, :, CONV_CTX - 1, :] = e
    y_ref[...] = (bg * y * _silu(gate)).astype(BF16)
    xn = xs + _dot(y_ref[...], w_out_ref[0])
    xo_ref[...] = xn
    _sample_queries(xn, gx_ref, wq_ref, q_ref)


def _odd_sample(xs, prev, layer, j, state, carry, norm_g, w_in, conv_w, w_out, norm_xg, w_q, w_o):
    n, d = xs.shape
    w_c = conv_w.shape[-1]
    whole = pl.BlockSpec((n, d), lambda i: (0, 0))
    stspec = pl.BlockSpec((1, n, CONV_CTX, w_c), lambda i: (j, 0, 0, 0))
    qshape = (n, d // LANES, LANES)
    first = prev is None
    head_specs = [whole] + ([] if first else _prev_attention_specs(prev, w_o))
    head_args = (xs,) if first else (xs, prev, w_o)
    carry_specs, carry_args = _carry_operand(carry)
    return pl.pallas_call(
        functools.partial(_odd_sample_kernel, first=first, n_carry=j),
        grid=(1,),
        in_specs=head_specs + [
            stspec, _layer_block(norm_g, layer), _layer_block(w_in, 0), _layer_block(conv_w, j),
            _layer_block(w_out, 0), _layer_block(norm_xg, layer), _layer_block(w_q, 0)] + carry_specs,
        out_specs=[whole, pl.BlockSpec((j + 1, n, CONV_CTX, w_c), lambda i: (0, 0, 0, 0)),
                   pl.BlockSpec(qshape, lambda i: (0, 0, 0))],
        out_shape=[jax.ShapeDtypeStruct((n, d), F32),
                   jax.ShapeDtypeStruct((j + 1, n, CONV_CTX, w_c), F32),
                   jax.ShapeDtypeStruct(qshape, F32)],
        scratch_shapes=[pltpu.VMEM((n, w_c), BF16)],
        compiler_params=_params(1),
        name="odd_sample",
    )(*head_args, state, norm_g, w_in, conv_w, w_out, norm_xg, w_q, *carry_args)


def _sample_finish_kernel(xs_ref, o_ref, wo_ref, gf_ref, y_ref):
    y_ref[...] = _rms(_sample_residual(xs_ref[...], o_ref, wo_ref), gf_ref[...])


def _sample_finish(xs, prev, w_o, final_g):
    n, d = xs.shape
    whole = pl.BlockSpec((n, d), lambda i: (0, 0))
    return pl.pallas_call(
        _sample_finish_kernel,
        grid=(1,),
        in_specs=[whole] + _prev_attention_specs(prev, w_o)
                 + [pl.BlockSpec((1, d), lambda i: (0, 0))],
        out_specs=whole,
        out_shape=jax.ShapeDtypeStruct((n, d), F32),
        compiler_params=_params(1),
        name="sample_finish",
    )(xs, prev, w_o, final_g)


def kernel(x_prompt, x_sample, mem_prompt, state_pool, state_conv, cache_mem_k, cache_mem_v, norm_mix_g, norm_xattn_g, norm_mem_g, w_in_ab, pool_maps, pool_scale, sgu_w, sgu_b, sgu_g, w_out_ab, w_in_c, conv_w, w_out_c, w_q, w_k, w_v, w_o, norm_final_g):
    depth, d = norm_mix_g.shape
    n_s, dec_seq, _ = x_sample.shape
    n_even, n_odd = pool_scale.shape[0], conv_w.shape[0]
    w_a, w_b, w_c = pool_scale.shape[-1], sgu_g.shape[-1], conv_w.shape[-1]
    hd = d // XA_HEADS
    assert dec_seq == 1, "sample group is one new token per sequence"
    assert PAST_LEN % CHUNK == 0, "the sample token must open a spatial-gating chunk"
    assert (hd // LANES) * XA_HEADS == SUBLANES, "one memory token per (8, 128) register tile"

    maps_rows = pool_maps.reshape(n_even, w_a, w_a // POOL_GROUPS)

    def matmul_weights(i):
        j = i // 2
        mixer = [(w_in_ab, j), (w_out_ab, j), (maps_rows, j)] if i % 2 == 0 else [(w_in_c, j), (w_out_c, j)]
        return mixer + [(w_q, i), (w_o, i)]

    row = lambda a: a.reshape(a.shape[0], 1, a.shape[-1])
    g_mix, g_xattn, g_mem = row(norm_mix_g), row(norm_xattn_g), row(norm_mem_g)
    pscale, sgg = row(pool_scale), row(sgu_g)
    g_final = norm_final_g.reshape(1, d)
    sw = w_b // SGU_GROUPS
    sgbias = jnp.repeat(jnp.swapaxes(sgu_b, 1, 2), sw, axis=2)
    w00 = row(jnp.repeat(sgu_w[:, :, 0, 0], sw, axis=1))
    b0 = row(jnp.repeat(sgu_b[:, :, 0], sw, axis=1))

    (k_p, v_p, kt_p, vb_p), wts = _memory_kv(mem_prompt, g_mem, w_k, w_v, matmul_weights(0))
    cache_k = _lane_tiled(cache_mem_k)
    cache_v = _lane_tiled(cache_mem_v)
    pool_state = jnp.swapaxes(state_pool, 1, 2)

    xp = x_prompt
    xs = x_sample.reshape(n_s, d)
    so = wo_prev = None
    pool_s = conv_s = None
    pool_p, conv_p, vrows_s = [], [], []
    for i in range(depth):
        j = i // 2
        if i % 2 == 0:
            w_in_b, w_out_b, maps_b, wq_b, wo_b = wts
            maps_b = maps_b.reshape(1, POOL_GROUPS, w_a // POOL_GROUPS, w_a // POOL_GROUPS)
            xs, pool_s, vn, sq = _even_sample(xs, so, i, j, pool_state, pool_s, g_mix, w_in_b, maps_b,
                                                pscale, w00, b0, sgg, w_out_b, g_xattn, wq_b,
                                                wo_prev, PAST_LEN)
            vrows_s.append(vn)
            xp, tail, so = _even_prompt(xp, i, j, g_mix, w_in_b, maps_b, pscale, sgu_w, sgbias, sgg,
                                        w_out_b, sq, cache_k, cache_v)
            pool_p.append(tail[:, POOL_HIST - POOL_CTX:])
        else:
            w_in_b, w_out_b, wq_b, wo_b = wts
            xs, conv_s, sq = _odd_sample(xs, so, i, j, state_conv, conv_s, g_mix, w_in_b, conv_w,
                                           w_out_b, g_xattn, wq_b, wo_prev)
            xp, tail, so = _odd_prompt(xp, i, j, g_mix, w_in_b, conv_w, w_out_b, sq, cache_k, cache_v)
            conv_p.append(tail[:, CONV_HIST - CONV_CTX:])
        so = so.reshape(n_s * so.shape[1], LANES)
        wo_prev = wo_b
        xp, wts = _attn_prompt(xp, i, g_xattn, wq_b, kt_p, vb_p, wo_b, g_final, i == depth - 1,
                               matmul_weights(i + 1) if i + 1 < depth else [])
    ys = _sample_finish(xs, so, wo_prev, g_final)

    new_pool_s = jnp.swapaxes(pool_s, 1, 2)
    new_vrows = jnp.concatenate(vrows_s, axis=0).reshape(n_even, n_s, 1, w_b)
    return (xp, ys.reshape(n_s, 1, d), jnp.stack(pool_p), new_pool_s, jnp.stack(conv_p),
            conv_s, new_vrows, _from_lane_tiled(k_p), _from_lane_tiled(v_p))
```

```python
import functools

import jax
import jax.numpy as jnp
from jax import lax
from jax.experimental import pallas as pl
from jax.experimental.pallas import tpu as pltpu

POOL_WINDOWS = (2, 4, 8, 16)
POOL_GROUPS = len(POOL_WINDOWS)
POOL_CTX = max(POOL_WINDOWS) - 1
SGU_GROUPS = 4
CHUNK = 128
CONV_WIDTH = 3
CONV_CTX = CONV_WIDTH - 1
XA_HEADS = 4
EPS = 1e-6
PAST_LEN = 16384

V7X_VMEM_LIMIT_BYTES = 60 * 1024 * 1024
SUBLANES = 8
LANES = 128
POOL_HIST = 16
CONV_HIST = 8

PROMPT_ROWS = 512
CONV_COL_SLAB = 512
ATTN_ROWS = 1024
MEMKV_BATCH = 2
SAMPLE_TOKEN_BLOCK = 16

F32 = jnp.float32
BF16 = jnp.bfloat16


def _rms(x, g):
    return x * lax.rsqrt(jnp.mean(x * x, axis=-1, keepdims=True) + EPS) * g


def _silu(x):
    return x * jax.nn.sigmoid(x)


def _dot(a, b):
    return jnp.dot(a, b, preferred_element_type=F32)


def _layer_block(arr, layer):
    index = (layer,) + (0,) * (arr.ndim - 1)
    return pl.BlockSpec((1,) + arr.shape[1:], lambda *_: index, pipeline_mode=pl.Buffered(1))


def _params(n_grid):
    return pltpu.CompilerParams(dimension_semantics=("arbitrary",) * n_grid,
                                vmem_limit_bytes=V7X_VMEM_LIMIT_BYTES)


def _tile_row_feature(r, hd):
    return (r % XA_HEADS) * hd + (r // XA_HEADS) * LANES


def _lane_tiled(cache):
    l, b, m, h, hd = cache.shape
    t = hd // LANES
    return cache.reshape(l, b, m, h, t, LANES).transpose(0, 1, 2, 4, 3, 5).reshape(l, b, m, t * h, LANES)


def _from_lane_tiled(tiled):
    l, b, m, r, lanes = tiled.shape
    t = r // XA_HEADS
    return tiled.reshape(l, b, m, t, XA_HEADS, lanes).transpose(0, 1, 2, 4, 3, 5).reshape(
        l, b, m, XA_HEADS, t * lanes)


def _to_tile_rows(a, rows):
    hd = a.shape[1] // XA_HEADS
    pieces = [a[:, _tile_row_feature(r, hd):_tile_row_feature(r, hd) + LANES] for r in range(rows)]
    return jnp.swapaxes(jnp.stack(pieces, axis=0), 0, 1)


def _cast_specs(to_cast, n_steps, step):
    ins, outs, shapes = [], [], []
    for w, l in to_cast:
        _, k, n = w.shape
        ins.append(pl.BlockSpec((1, k // n_steps, n), lambda i, s, l=l: (l, step(i, s), 0)))
        outs.append(pl.BlockSpec((1, k // n_steps, n), lambda i, s: (0, step(i, s), 0)))
        shapes.append(jax.ShapeDtypeStruct((1, k, n), BF16))
    return ins, outs, shapes


def _cast_slabs(srcs, dsts):
    for src, dst in zip(srcs, dsts):
        dst[...] = src[...].astype(BF16)


def _memkv_kernel(*refs, n_cast):
    mem_ref, g_ref, wk_ref, wv_ref = refs[:4]
    k_ref, v_ref, kt_ref, vb_ref = refs[4 + n_cast:8 + n_cast]
    _cast_slabs(refs[4:4 + n_cast], refs[8 + n_cast:])
    nb, m, d = mem_ref.shape
    rows = k_ref.shape[3]
    mn = _rms(mem_ref[...].reshape(nb * m, d), g_ref[0]).astype(BF16)
    k = _dot(mn, wk_ref[0].astype(BF16))
    v = _dot(mn, wv_ref[0].astype(BF16))
    for i in range(nb):
        tok = slice(i * m, (i + 1) * m)
        k_ref[0, i] = _to_tile_rows(k[tok], rows)
        v_ref[0, i] = _to_tile_rows(v[tok], rows)
        kt_ref[0, i] = k[tok].T.astype(BF16)
        vb_ref[0, i] = v[tok].astype(BF16)


def _memory_kv(mem, norm_g, w_k, w_v, to_cast):
    depth, _, d = norm_g.shape
    b, m, _ = mem.shape
    rows = d // LANES
    nb = MEMKV_BATCH
    wspec = pl.BlockSpec((1, d, d), lambda l, i: (l, 0, 0))
    tiled = pl.BlockSpec((1, nb, m, rows, LANES), lambda l, i: (l, i, 0, 0, 0))
    cast_in, cast_out, cast_shapes = _cast_specs(to_cast, depth * (b // nb), lambda l, i: l * (b // nb) + i)
    outs = pl.pallas_call(
        functools.partial(_memkv_kernel, n_cast=len(to_cast)),
        grid=(depth, b // nb),
        in_specs=[pl.BlockSpec((nb, m, d), lambda l, i: (i, 0, 0)),
                  pl.BlockSpec((1, 1, d), lambda l, i: (l, 0, 0)),
                  wspec, wspec] + cast_in,
        out_specs=[tiled, tiled,
                   pl.BlockSpec((1, nb, d, m), lambda l, i: (l, i, 0, 0)),
                   pl.BlockSpec((1, nb, m, d), lambda l, i: (l, i, 0, 0))] + cast_out,
        out_shape=[jax.ShapeDtypeStruct((depth, b, m, rows, LANES), F32),
                   jax.ShapeDtypeStruct((depth, b, m, rows, LANES), F32),
                   jax.ShapeDtypeStruct((depth, b, d, m), BF16),
                   jax.ShapeDtypeStruct((depth, b, m, d), BF16)] + cast_shapes,
        compiler_params=_params(2),
        name="memory_kv",
    )(mem, norm_g, w_k, w_v, *[w for w, _ in to_cast])
    return outs[:4], outs[4:]


def _sample_attention(q_ref, k_ref, v_ref, o_ref):
    n, rows, lanes = q_ref.shape
    m = k_ref.shape[2]
    hd = rows * lanes // XA_HEADS
    for i in range(n):
        qt = q_ref[i] * (hd ** -0.5)
        mx = den = acc = None
        for t0 in range(0, m, SAMPLE_TOKEN_BLOCK):
            tok = slice(t0, t0 + SAMPLE_TOKEN_BLOCK)
            part = jnp.sum(k_ref[0, i, tok] * qt[None], axis=-1, keepdims=True)
            s = part + pltpu.roll(part, XA_HEADS, 1)
            block_max = jnp.max(s, axis=0)
            new_mx = block_max if mx is None else jnp.maximum(mx, block_max)
            e = jnp.exp(s - new_mx)
            block_den = jnp.sum(e, axis=0)
            block_acc = jnp.sum(e * v_ref[0, i, tok], axis=0)
            if mx is None:
                den, acc = block_den, block_acc
            else:
                alpha = jnp.exp(mx - new_mx)
                den = den * alpha + block_den
                acc = acc * alpha + block_acc
            mx = new_mx
        o_ref[i] = acc / den


def _sample_attention_specs(q_tiles, cache_k, layer, n_steps):
    n, rows, lanes = q_tiles.shape
    m = cache_k.shape[2]
    per_step = n // (n_steps[0] * n_steps[1])
    step = lambda i, s: i * n_steps[1] + s
    qspec = pl.BlockSpec((per_step, rows, lanes), lambda i, s: (step(i, s), 0, 0))
    kvspec = pl.BlockSpec((1, per_step, m, rows, lanes), lambda i, s: (layer, step(i, s), 0, 0, 0))
    return qspec, kvspec


def _even_prompt_kernel(x_ref, g_ref, w_in_ref, maps_ref, pscale_ref, sgw_ref, sgbias_ref, sgg_ref,
                        w_out_ref, sq_ref, sk_ref, sv_ref, xo_ref, tail_ref, so_ref,
                        xa_ref, ga_ref, u_ref, v_ref, gb_ref, y_ref, hist_ref):
    rows, w_a = y_ref.shape[0], pscale_ref.shape[-1]
    w_b = sgg_ref.shape[-1]
    gw = w_a // POOL_GROUPS
    sw = w_b // SGU_GROUPS
    s = pl.program_id(1)

    @pl.when(s == 0)
    def _():
        hist_ref[...] = jnp.zeros_like(hist_ref)

    x = x_ref[0]
    h = _rms(x, g_ref[0]).astype(BF16)
    xa_ref[...] = _dot(h, w_in_ref[0, :, 0:w_a])
    v_ref[...] = _dot(h, w_in_ref[0, :, 2 * w_a + w_b:2 * w_a + 2 * w_b])

    pos = s * rows + lax.broadcasted_iota(jnp.int32, (rows, gw), 0)
    pooled = []
    for g, w in enumerate(POOL_WINDOWS):
        cols = slice(g * gw, (g + 1) * gw)
        xa = xa_ref[:, cols]
        acc = jnp.concatenate([hist_ref[:, cols], xa], axis=0)
        shift = 1
        while shift < w:
            acc = acc + pltpu.roll(acc, shift, 0)
            shift *= 2
        cnt = jnp.minimum(pos + 1, w).astype(F32)
        pooled.append((acc[POOL_HIST:] / cnt - xa).astype(BF16))
    tail = xa_ref[rows - POOL_HIST:, :]
    hist_ref[...] = tail
    tail_ref[0] = tail

    ga_ref[...] = _dot(h, w_in_ref[0, :, w_a:2 * w_a])
    for g in range(POOL_GROUPS):
        cols = slice(g * gw, (g + 1) * gw)
        ya = _dot(pooled[g], maps_ref[0, g]) * pscale_ref[0, :, cols] * _silu(ga_ref[:, cols])
        y_ref[:, cols] = ya.astype(BF16)

    u_ref[...] = _dot(h, w_in_ref[0, :, 2 * w_a:2 * w_a + w_b])
    gb_ref[...] = _dot(h, w_in_ref[0, :, 2 * w_a + 2 * w_b:])
    out = x + _dot(y_ref[:, 0:w_a], w_out_ref[0, 0:w_a, :])

    vn = _rms(v_ref[...], sgg_ref[0]).astype(BF16)
    causal = (lax.broadcasted_iota(jnp.int32, (CHUNK, CHUNK), 0)
              >= lax.broadcasted_iota(jnp.int32, (CHUNK, CHUNK), 1))
    for g in range(SGU_GROUPS):
        cols = slice(g * sw, (g + 1) * sw)
        wg = jnp.where(causal, sgw_ref[0, g], 0.0).astype(BF16)
        for c in range(rows // CHUNK):
            rs = slice(c * CHUNK, (c + 1) * CHUNK)
            mixed = _dot(wg, vn[rs, cols]) + sgbias_ref[0, :, cols]
            y_ref[rs, w_a + g * sw:w_a + (g + 1) * sw] = (
                u_ref[rs, cols] * mixed * _silu(gb_ref[rs, cols])).astype(BF16)

    xo_ref[0] = out + _dot(y_ref[:, w_a:], w_out_ref[0, w_a:, :])
    _sample_attention(sq_ref, sk_ref, sv_ref, so_ref)


def _even_prompt(x, layer, j, norm_g, w_in, maps, pscale, sgw, sgbias, sgg, w_out, sq, cache_k, cache_v):
    b, seq, d = x.shape
    w_a, w_b = pscale.shape[-1], sgg.shape[-1]
    rows = PROMPT_ROWS
    xspec = pl.BlockSpec((1, rows, d), lambda i, s: (i, s, 0))
    qspec, kvspec = _sample_attention_specs(sq, cache_k, layer, (b, seq // rows))
    return pl.pallas_call(
        _even_prompt_kernel,
        grid=(b, seq // rows),
        in_specs=[xspec, _layer_block(norm_g, layer), _layer_block(w_in, 0), _layer_block(maps, 0),
                  _layer_block(pscale, j), _layer_block(sgw, j), _layer_block(sgbias, j),
                  _layer_block(sgg, j), _layer_block(w_out, 0), qspec, kvspec, kvspec],
        out_specs=[xspec, pl.BlockSpec((1, POOL_HIST, w_a), lambda i, s: (i, 0, 0)), qspec],
        out_shape=[jax.ShapeDtypeStruct(x.shape, F32),
                   jax.ShapeDtypeStruct((b, POOL_HIST, w_a), F32),
                   jax.ShapeDtypeStruct(sq.shape, F32)],
        scratch_shapes=[pltpu.VMEM((rows, w_a), F32), pltpu.VMEM((rows, w_a), F32),
                        pltpu.VMEM((rows, w_b), F32), pltpu.VMEM((rows, w_b), F32),
                        pltpu.VMEM((rows, w_b), F32),
                        pltpu.VMEM((rows, w_a + w_b), BF16),
                        pltpu.VMEM((POOL_HIST, w_a), F32)],
        compiler_params=_params(2),
        name="even_prompt",
    )(x, norm_g, w_in, maps, pscale, sgw, sgbias, sgg, w_out, sq, cache_k, cache_v)


def _odd_prompt_kernel(x_ref, g_ref, w_in_ref, cw_ref, w_out_ref, sq_ref, sk_ref, sv_ref,
                       xo_ref, tail_ref, so_ref, y_ref, hist_ref, *, col_slab):
    rows, w_c = y_ref.shape
    s = pl.program_id(1)

    @pl.when(s == 0)
    def _():
        hist_ref[...] = jnp.zeros_like(hist_ref)

    x = x_ref[0]
    h = _rms(x, g_ref[0]).astype(BF16)
    for c0 in range(0, w_c, col_slab):
        cols = slice(c0, c0 + col_slab)
        zb = [_dot(h, w_in_ref[0, :, part * w_c + c0:part * w_c + c0 + col_slab]) for part in range(4)]
        e = zb[1] * zb[2]
        ext = jnp.concatenate([hist_ref[:, cols], e], axis=0)
        y = (cw_ref[0, 0:1, cols] * pltpu.roll(ext, 2, 0)[CONV_HIST:]
             + cw_ref[0, 1:2, cols] * pltpu.roll(ext, 1, 0)[CONV_HIST:]
             + cw_ref[0, 2:3, cols] * e)
        tail = e[rows - CONV_HIST:]
        hist_ref[:, cols] = tail
        tail_ref[0, :, cols] = tail
        y_ref[:, cols] = (zb[0] * y * _silu(zb[3])).astype(BF16)
    xo_ref[0] = x + _dot(y_ref[...], w_out_ref[0])
    _sample_attention(sq_ref, sk_ref, sv_ref, so_ref)


def _odd_prompt(x, layer, j, norm_g, w_in, conv_w, w_out, sq, cache_k, cache_v):
    b, seq, d = x.shape
    w_c = conv_w.shape[-1]
    rows = PROMPT_ROWS
    xspec = pl.BlockSpec((1, rows, d), lambda i, s: (i, s, 0))
    qspec, kvspec = _sample_attention_specs(sq, cache_k, layer, (b, seq // rows))
    return pl.pallas_call(
        functools.partial(_odd_prompt_kernel, col_slab=CONV_COL_SLAB),
        grid=(b, seq // rows),
        in_specs=[xspec, _layer_block(norm_g, layer), _layer_block(w_in, 0), _layer_block(conv_w, j),
                  _layer_block(w_out, 0), qspec, kvspec, kvspec],
        out_specs=[xspec, pl.BlockSpec((1, CONV_HIST, w_c), lambda i, s: (i, 0, 0)), qspec],
        out_shape=[jax.ShapeDtypeStruct(x.shape, F32),
                   jax.ShapeDtypeStruct((b, CONV_HIST, w_c), F32),
                   jax.ShapeDtypeStruct(sq.shape, F32)],
        scratch_shapes=[pltpu.VMEM((rows, w_c), BF16),
                        pltpu.VMEM((CONV_HIST, w_c), F32)],
        compiler_params=_params(2),
        name="odd_prompt",
    )(x, norm_g, w_in, conv_w, w_out, sq, cache_k, cache_v)


def _attn_prompt_kernel(*refs, final, n_cast):
    x_ref, g_ref, wq_ref, kt_ref, v_ref, wo_ref, gf_ref = refs[:7]
    cast_src = refs[7:7 + n_cast]
    xo_ref = refs[7 + n_cast]
    cast_dst = refs[8 + n_cast:8 + 2 * n_cast]
    o_ref = refs[8 + 2 * n_cast]
    _cast_slabs(cast_src, cast_dst)
    d = x_ref.shape[-1]
    hd = d // XA_HEADS
    x = x_ref[0]
    h = _rms(x, g_ref[0]).astype(BF16)
    q = _dot(h, wq_ref[0]).astype(BF16)
    heads = [slice(hh * hd, (hh + 1) * hd) for hh in range(XA_HEADS)]
    scores = [_dot(q[:, cols], kt_ref[0, 0, cols, :]) * (hd ** -0.5) for cols in heads]
    for cols, s in zip(heads, scores):
        e = jnp.exp(s - jnp.max(s, axis=-1, keepdims=True))
        p = e / jnp.sum(e, axis=-1, keepdims=True)
        o_ref[:, cols] = _dot(p.astype(BF16), v_ref[0, 0, :, cols]).astype(BF16)
    xn = x + _dot(o_ref[...], wo_ref[0])
    if final:
        xn = _rms(xn, gf_ref[...])
    xo_ref[0] = xn


def _attn_prompt(x, layer, norm_g, w_q, kt, vb, w_o, final_g, final, to_cast):
    b, seq, d = x.shape
    m = vb.shape[2]
    rows = ATTN_ROWS
    n_steps = b * (seq // rows)
    xspec = pl.BlockSpec((1, rows, d), lambda i, s: (i, s, 0))
    cast_in, cast_out, cast_shapes = _cast_specs(to_cast, n_steps, lambda i, s: i * (seq // rows) + s)
    outs = pl.pallas_call(
        functools.partial(_attn_prompt_kernel, final=final, n_cast=len(to_cast)),
        grid=(b, seq // rows),
        in_specs=[xspec, _layer_block(norm_g, layer), _layer_block(w_q, 0),
                  pl.BlockSpec((1, 1, d, m), lambda i, s: (layer, i, 0, 0)),
                  pl.BlockSpec((1, 1, m, d), lambda i, s: (layer, i, 0, 0)),
                  _layer_block(w_o, 0),
                  pl.BlockSpec((1, d), lambda i, s: (0, 0), pipeline_mode=pl.Buffered(1))] + cast_in,
        out_specs=[xspec] + cast_out,
        out_shape=[jax.ShapeDtypeStruct(x.shape, F32)] + cast_shapes,
        scratch_shapes=[pltpu.VMEM((rows, d), BF16)],
        compiler_params=_params(2),
        name="attn_prompt",
    )(x, norm_g, w_q, kt, vb, w_o, final_g, *[w for w, _ in to_cast])
    return outs[0], outs[1:]


def _sample_enter(refs, first):
    if first:
        return refs[0][...], refs[1:]
    xs_ref, o_ref, wo_ref = refs[:3]
    return _sample_residual(xs_ref[...], o_ref, wo_ref), refs[3:]


def _sample_residual(xs, o_ref, wo_ref):
    n, d = xs.shape
    rows = o_ref.shape[0] // n
    hd = d // XA_HEADS
    acc = xs
    for r in range(rows):
        f = _tile_row_feature(r, hd)
        o_r = o_ref[pl.ds(r, n, stride=rows), :].astype(BF16)
        acc = acc + _dot(o_r, wo_ref[0, f:f + LANES, :])
    return acc


def _sample_queries(xs, g_ref, wq_ref, q_ref):
    q = _dot(_rms(xs, g_ref[0]).astype(BF16), wq_ref[0])
    q_ref[...] = _to_tile_rows(q, q_ref.shape[1])


def _carry_operand(carry):
    if carry is None:
        return [], ()
    zeros = (0,) * carry.ndim
    return [pl.BlockSpec(carry.shape, lambda i: zeros)], (carry,)


def _split_carry(refs, n_carry):
    return (refs[0], refs[1:]) if n_carry else (None, refs)


def _prev_attention_specs(o_rows, w_o_prev):
    return [pl.BlockSpec(o_rows.shape, lambda i: (0, 0)), _layer_block(w_o_prev, 0)]


def _even_sample_kernel(*refs, pos0, first, n_carry):
    xs, refs = _sample_enter(refs, first)
    (st_ref, g_ref, w_in_ref, maps_ref, pscale_ref, w00_ref, b0_ref, sgg_ref, w_out_ref, gx_ref,
     wq_ref) = refs[:11]
    carry_ref, refs = _split_carry(refs[11:], n_carry)
    xo_ref, pool_ref, vn_ref, q_ref, y_ref = refs
    w_a, w_b = pscale_ref.shape[-1], sgg_ref.shape[-1]
    gw = w_a // POOL_GROUPS
    h = _rms(xs, g_ref[0]).astype(BF16)
    xa = _dot(h, w_in_ref[0, :, 0:w_a])
    ga = _dot(h, w_in_ref[0, :, w_a:2 * w_a])
    u = _dot(h, w_in_ref[0, :, 2 * w_a:2 * w_a + w_b])
    v = _dot(h, w_in_ref[0, :, 2 * w_a + w_b:2 * w_a + 2 * w_b])
    gb = _dot(h, w_in_ref[0, :, 2 * w_a + 2 * w_b:])

    for g, w in enumerate(POOL_WINDOWS):
        cols = slice(g * gw, (g + 1) * gw)
        acc = xa[:, cols]
        for k in range(POOL_CTX - (w - 1), POOL_CTX):
            acc = acc + st_ref[0, k, :, cols]
        pooled = acc / float(min(pos0 + 1, w)) - xa[:, cols]
        ya = _dot(pooled.astype(BF16), maps_ref[0, g]) * pscale_ref[0, :, cols] * _silu(ga[:, cols])
        y_ref[:, cols] = ya.astype(BF16)
    if n_carry:
        pool_ref[0:n_carry] = carry_ref[...]
    pool_ref[n_carry, 0:POOL_CTX - 1] = st_ref[0, 1:POOL_CTX]
    pool_ref[n_carry, POOL_CTX - 1] = xa

    vn = _rms(v, sgg_ref[0])
    vn_ref[0] = vn
    mixed = w00_ref[0] * vn + b0_ref[0]
    y_ref[:, w_a:] = (u * mixed * _silu(gb)).astype(BF16)
    xn = xs + _dot(y_ref[...], w_out_ref[0])
    xo_ref[...] = xn
    _sample_queries(xn, gx_ref, wq_ref, q_ref)


def _even_sample(xs, prev, layer, j, state, carry, norm_g, w_in, maps, pscale, w00, b0, sgg, w_out,
                 norm_xg, w_q, w_o, pos0):
    n, d = xs.shape
    _, ctx, _, w_a = state.shape
    w_b = sgg.shape[-1]
    whole = pl.BlockSpec((n, d), lambda i: (0, 0))
    qshape = (n, d // LANES, LANES)
    first = prev is None
    head_specs = [whole] + ([] if first else _prev_attention_specs(prev, w_o))
    head_args = (xs,) if first else (xs, prev, w_o)
    carry_specs, carry_args = _carry_operand(carry)
    return pl.pallas_call(
        functools.partial(_even_sample_kernel, pos0=pos0, first=first, n_carry=j),
        grid=(1,),
        in_specs=head_specs + [
            _layer_block(state, j), _layer_block(norm_g, layer), _layer_block(w_in, 0),
            _layer_block(maps, 0), _layer_block(pscale, j), _layer_block(w00, j), _layer_block(b0, j),
            _layer_block(sgg, j), _layer_block(w_out, 0), _layer_block(norm_xg, layer),
            _layer_block(w_q, 0)] + carry_specs,
        out_specs=[whole, pl.BlockSpec((j + 1, ctx, n, w_a), lambda i: (0, 0, 0, 0)),
                   pl.BlockSpec((1, n, w_b), lambda i: (0, 0, 0)),
                   pl.BlockSpec(qshape, lambda i: (0, 0, 0))],
        out_shape=[jax.ShapeDtypeStruct((n, d), F32),
                   jax.ShapeDtypeStruct((j + 1, ctx, n, w_a), F32),
                   jax.ShapeDtypeStruct((1, n, w_b), F32),
                   jax.ShapeDtypeStruct(qshape, F32)],
        scratch_shapes=[pltpu.VMEM((n, w_a + w_b), BF16)],
        compiler_params=_params(1),
        name="even_sample",
    )(*head_args, state, norm_g, w_in, maps, pscale, w00, b0, sgg, w_out, norm_xg, w_q, *carry_args)


def _odd_sample_kernel(*refs, first, n_carry):
    xs, refs = _sample_enter(refs, first)
    st_ref, g_ref, w_in_ref, cw_ref, w_out_ref, gx_ref, wq_ref = refs[:7]
    carry_ref, refs = _split_carry(refs[7:], n_carry)
    xo_ref, conv_ref, q_ref, y_ref = refs
    w_c = cw_ref.shape[-1]
    h = _rms(xs, g_ref[0]).astype(BF16)
    bg = _dot(h, w_in_ref[0, :, 0:w_c])
    cg = _dot(h, w_in_ref[0, :, w_c:2 * w_c])
    xc = _dot(h, w_in_ref[0, :, 2 * w_c:3 * w_c])
    gate = _dot(h, w_in_ref[0, :, 3 * w_c:])
    e = cg * xc
    newest = st_ref[0, :, CONV_CTX - 1, :]
    y = cw_ref[0, 0:1, :] * st_ref[0, :, 0, :] + cw_ref[0, 1:2, :] * newest + cw_ref[0, 2:3, :] * e
    if n_carry:
        conv_ref[0:n_carry] = carry_ref[...]
    conv_ref[n_carry, :, 0, :] = newest
    conv_ref[n_carry, :, CONV_CTX - 1, :] = e
    y_ref[...] = (bg * y * _silu(gate)).astype(BF16)
    xn = xs + _dot(y_ref[...], w_out_ref[0])
    xo_ref[...] = xn
    _sample_queries(xn, gx_ref, wq_ref, q_ref)


def _odd_sample(xs, prev, layer, j, state, carry, norm_g, w_in, conv_w, w_out, norm_xg, w_q, w_o):
    n, d = xs.shape
    w_c = conv_w.shape[-1]
    whole = pl.BlockSpec((n, d), lambda i: (0, 0))
    stspec = pl.BlockSpec((1, n, CONV_CTX, w_c), lambda i: (j, 0, 0, 0))
    qshape = (n, d // LANES, LANES)
    first = prev is None
    head_specs = [whole] + ([] if first else _prev_attention_specs(prev, w_o))
    head_args = (xs,) if first else (xs, prev, w_o)
    carry_specs, carry_args = _carry_operand(carry)
    return pl.pallas_call(
        functools.partial(_odd_sample_kernel, first=first, n_carry=j),
        grid=(1,),
        in_specs=head_specs + [
            stspec, _layer_block(norm_g, layer), _layer_block(w_in, 0), _layer_block(conv_w, j),
            _layer_block(w_out, 0), _layer_block(norm_xg, layer), _layer_block(w_q, 0)] + carry_specs,
        out_specs=[whole, pl.BlockSpec((j + 1, n, CONV_CTX, w_c), lambda i: (0, 0, 0, 0)),
                   pl.BlockSpec(qshape, lambda i: (0, 0, 0))],
        out_shape=[jax.ShapeDtypeStruct((n, d), F32),
                   jax.ShapeDtypeStruct((j + 1, n, CONV_CTX, w_c), F32),
                   jax.ShapeDtypeStruct(qshape, F32)],
        scratch_shapes=[pltpu.VMEM((n, w_c), BF16)],
        compiler_params=_params(1),
        name="odd_sample",
    )(*head_args, state, norm_g, w_in, conv_w, w_out, norm_xg, w_q, *carry_args)


def _sample_finish_kernel(xs_ref, o_ref, wo_ref, gf_ref, y_ref):
    y_ref[...] = _rms(_sample_residual(xs_ref[...], o_ref, wo_ref), gf_ref[...])


def _sample_finish(xs, prev, w_o, final_g):
    n, d = xs.shape
    whole = pl.BlockSpec((n, d), lambda i: (0, 0))
    return pl.pallas_call(
        _sample_finish_kernel,
        grid=(1,),
        in_specs=[whole] + _prev_attention_specs(prev, w_o)
                 + [pl.BlockSpec((1, d), lambda i: (0, 0))],
        out_specs=whole,
        out_shape=jax.ShapeDtypeStruct((n, d), F32),
        compiler_params=_params(1),
        name="sample_finish",
    )(xs, prev, w_o, final_g)


def kernel(x_prompt, x_sample, mem_prompt, state_pool, state_conv, cache_mem_k, cache_mem_v, norm_mix_g, norm_xattn_g, norm_mem_g, w_in_ab, pool_maps, pool_scale, sgu_w, sgu_b, sgu_g, w_out_ab, w_in_c, conv_w, w_out_c, w_q, w_k, w_v, w_o, norm_final_g):
    depth, d = norm_mix_g.shape
    n_s, dec_seq, _ = x_sample.shape
    n_even, n_odd = pool_scale.shape[0], conv_w.shape[0]
    w_a, w_b, w_c = pool_scale.shape[-1], sgu_g.shape[-1], conv_w.shape[-1]
    hd = d // XA_HEADS
    assert dec_seq == 1, "sample group is one new token per sequence"
    assert PAST_LEN % CHUNK == 0, "the sample token must open a spatial-gating chunk"
    assert (hd // LANES) * XA_HEADS == SUBLANES, "one memory token per (8, 128) register tile"

    maps_rows = pool_maps.reshape(n_even, w_a, w_a // POOL_GROUPS)

    def matmul_weights(i):
        j = i // 2
        mixer = [(w_in_ab, j), (w_out_ab, j), (maps_rows, j)] if i % 2 == 0 else [(w_in_c, j), (w_out_c, j)]
        return mixer + [(w_q, i), (w_o, i)]

    row = lambda a: a.reshape(a.shape[0], 1, a.shape[-1])
    g_mix, g_xattn, g_mem = row(norm_mix_g), row(norm_xattn_g), row(norm_mem_g)
    pscale, sgg = row(pool_scale), row(sgu_g)
    g_final = norm_final_g.reshape(1, d)
    sw = w_b // SGU_GROUPS
    sgbias = jnp.repeat(jnp.swapaxes(sgu_b, 1, 2), sw, axis=2)
    w00 = row(jnp.repeat(sgu_w[:, :, 0, 0], sw, axis=1))
    b0 = row(jnp.repeat(sgu_b[:, :, 0], sw, axis=1))

    (k_p, v_p, kt_p, vb_p), wts = _memory_kv(mem_prompt, g_mem, w_k, w_v, matmul_weights(0))
    cache_k = _lane_tiled(cache_mem_k)
    cache_v = _lane_tiled(cache_mem_v)
    pool_state = jnp.swapaxes(state_pool, 1, 2)

    xp = x_prompt
    xs = x_sample.reshape(n_s, d)
    so = wo_prev = None
    pool_s = conv_s = None
    pool_p, conv_p, vrows_s = [], [], []
    for i in range(depth):
        j = i // 2
        if i % 2 == 0:
            w_in_b, w_out_b, maps_b, wq_b, wo_b = wts
            maps_b = maps_b.reshape(1, POOL_GROUPS, w_a // POOL_GROUPS, w_a // POOL_GROUPS)
            xs, pool_s, vn, sq = _even_sample(xs, so, i, j, pool_state, pool_s, g_mix, w_in_b, maps_b,
                                                pscale, w00, b0, sgg, w_out_b, g_xattn, wq_b,
                                                wo_prev, PAST_LEN)
            vrows_s.append(vn)
            xp, tail, so = _even_prompt(xp, i, j, g_mix, w_in_b, maps_b, pscale, sgu_w, sgbias, sgg,
                                        w_out_b, sq, cache_k, cache_v)
            pool_p.append(tail[:, POOL_HIST - POOL_CTX:])
        else:
            w_in_b, w_out_b, wq_b, wo_b = wts
            xs, conv_s, sq = _odd_sample(xs, so, i, j, state_conv, conv_s, g_mix, w_in_b, conv_w,
                                           w_out_b, g_xattn, wq_b, wo_prev)
            xp, tail, so = _odd_prompt(xp, i, j, g_mix, w_in_b, conv_w, w_out_b, sq, cache_k, cache_v)
            conv_p.append(tail[:, CONV_HIST - CONV_CTX:])
        so = so.reshape(n_s * so.shape[1], LANES)
        wo_prev = wo_b
        xp, wts = _attn_prompt(xp, i, g_xattn, wq_b, kt_p, vb_p, wo_b, g_final, i == depth - 1,
                               matmul_weights(i + 1) if i + 1 < depth else [])
    ys = _sample_finish(xs, so, wo_prev, g_final)

    new_pool_s = jnp.swapaxes(pool_s, 1, 2)
    new_vrows = jnp.concatenate(vrows_s, axis=0).reshape(n_even, n_s, 1, w_b)
    return (xp, ys.reshape(n_s, 1, d), jnp.stack(pool_p), new_pool_s, jnp.stack(conv_p),
            conv_s, new_vrows, _from_lane_tiled(k_p), _from_lane_tiled(v_p))
```

```python
import functools

import jax
import jax.numpy as jnp
from jax import lax
from jax.experimental import pallas as pl
from jax.experimental.pallas import tpu as pltpu

POOL_WINDOWS = (2, 4, 8, 16)
POOL_GROUPS = len(POOL_WINDOWS)
POOL_CTX = max(POOL_WINDOWS) - 1
SGU_GROUPS = 4
CHUNK = 128
CONV_WIDTH = 3
CONV_CTX = CONV_WIDTH - 1
XA_HEADS = 4
EPS = 1e-6
PAST_LEN = 16384

V7X_VMEM_LIMIT_BYTES = 60 * 1024 * 1024
SUBLANES = 8
LANES = 128
POOL_HIST = 16
CONV_HIST = 8

PROMPT_ROWS = 512
ATTN_ROWS = 1024
MEMKV_BATCH = 2
SAMPLE_TOKEN_BLOCK = 32

F32 = jnp.float32
BF16 = jnp.bfloat16


def _rms(x, g):
    return x * lax.rsqrt(jnp.mean(x * x, axis=-1, keepdims=True) + EPS) * g


def _silu(x):
    return x * jax.nn.sigmoid(x)


def _dot(a, b):
    return jnp.dot(a, b, preferred_element_type=F32)


def _layer_block(arr, layer):
    index = (layer,) + (0,) * (arr.ndim - 1)
    return pl.BlockSpec((1,) + arr.shape[1:], lambda *_: index, pipeline_mode=pl.Buffered(1))


def _params(n_grid):
    return pltpu.CompilerParams(dimension_semantics=("arbitrary",) * n_grid,
                                vmem_limit_bytes=V7X_VMEM_LIMIT_BYTES)


def _tile_row_feature(r, hd):
    return (r % XA_HEADS) * hd + (r // XA_HEADS) * LANES


def _lane_tiled(cache):
    l, b, m, h, hd = cache.shape
    t = hd // LANES
    return cache.reshape(l, b, m, h, t, LANES).transpose(0, 1, 2, 4, 3, 5).reshape(l, b, m, t * h, LANES)


def _from_lane_tiled(tiled):
    l, b, m, r, lanes = tiled.shape
    t = r // XA_HEADS
    return tiled.reshape(l, b, m, t, XA_HEADS, lanes).transpose(0, 1, 2, 4, 3, 5).reshape(
        l, b, m, XA_HEADS, t * lanes)


def _to_tile_rows(a, rows):
    hd = a.shape[1] // XA_HEADS
    pieces = [a[:, _tile_row_feature(r, hd):_tile_row_feature(r, hd) + LANES] for r in range(rows)]
    return jnp.swapaxes(jnp.stack(pieces, axis=0), 0, 1)


def _cast_specs(to_cast, n_steps, step):
    ins, outs, shapes = [], [], []
    for w, l in to_cast:
        _, k, n = w.shape
        ins.append(pl.BlockSpec((1, k // n_steps, n), lambda i, s, l=l: (l, step(i, s), 0)))
        outs.append(pl.BlockSpec((1, k // n_steps, n), lambda i, s: (0, step(i, s), 0)))
        shapes.append(jax.ShapeDtypeStruct((1, k, n), BF16))
    return ins, outs, shapes


def _cast_slabs(srcs, dsts):
    for src, dst in zip(srcs, dsts):
        dst[...] = src[...].astype(BF16)


def _memkv_kernel(*refs, n_cast):
    mem_ref, g_ref, wk_ref, wv_ref = refs[:4]
    k_ref, v_ref, kt_ref, vb_ref = refs[4 + n_cast:8 + n_cast]
    _cast_slabs(refs[4:4 + n_cast], refs[8 + n_cast:])
    nb, m, d = mem_ref.shape
    rows = k_ref.shape[3]
    mn = _rms(mem_ref[...].reshape(nb * m, d), g_ref[0]).astype(BF16)
    k = _dot(mn, wk_ref[0].astype(BF16))
    v = _dot(mn, wv_ref[0].astype(BF16))
    for i in range(nb):
        tok = slice(i * m, (i + 1) * m)
        k_ref[0, i] = _to_tile_rows(k[tok], rows)
        v_ref[0, i] = _to_tile_rows(v[tok], rows)
        kt_ref[0, i] = k[tok].T.astype(BF16)
        vb_ref[0, i] = v[tok].astype(BF16)


def _memory_kv(mem, norm_g, w_k, w_v, to_cast):
    depth, _, d = norm_g.shape
    b, m, _ = mem.shape
    rows = d // LANES
    nb = MEMKV_BATCH
    wspec = pl.BlockSpec((1, d, d), lambda l, i: (l, 0, 0))
    tiled = pl.BlockSpec((1, nb, m, rows, LANES), lambda l, i: (l, i, 0, 0, 0))
    cast_in, cast_out, cast_shapes = _cast_specs(to_cast, depth * (b // nb), lambda l, i: l * (b // nb) + i)
    outs = pl.pallas_call(
        functools.partial(_memkv_kernel, n_cast=len(to_cast)),
        grid=(depth, b // nb),
        in_specs=[pl.BlockSpec((nb, m, d), lambda l, i: (i, 0, 0)),
                  pl.BlockSpec((1, 1, d), lambda l, i: (l, 0, 0)),
                  wspec, wspec] + cast_in,
        out_specs=[tiled, tiled,
                   pl.BlockSpec((1, nb, d, m), lambda l, i: (l, i, 0, 0)),
                   pl.BlockSpec((1, nb, m, d), lambda l, i: (l, i, 0, 0))] + cast_out,
        out_shape=[jax.ShapeDtypeStruct((depth, b, m, rows, LANES), F32),
                   jax.ShapeDtypeStruct((depth, b, m, rows, LANES), F32),
                   jax.ShapeDtypeStruct((depth, b, d, m), BF16),
                   jax.ShapeDtypeStruct((depth, b, m, d), BF16)] + cast_shapes,
        compiler_params=_params(2),
        name="memory_kv",
    )(mem, norm_g, w_k, w_v, *[w for w, _ in to_cast])
    return outs[:4], outs[4:]


def _sample_attention(q_ref, k_ref, v_ref, o_ref):
    n, rows, lanes = q_ref.shape
    m = k_ref.shape[2]
    hd = rows * lanes // XA_HEADS
    for i in range(n):
        qt = q_ref[i] * (hd ** -0.5)
        mx = den = acc = None
        for t0 in range(0, m, SAMPLE_TOKEN_BLOCK):
            tok = slice(t0, t0 + SAMPLE_TOKEN_BLOCK)
            part = jnp.sum(k_ref[0, i, tok] * qt[None], axis=-1, keepdims=True)
            s = part + pltpu.roll(part, XA_HEADS, 1)
            block_max = jnp.max(s, axis=0)
            new_mx = block_max if mx is None else jnp.maximum(mx, block_max)
            e = jnp.exp(s - new_mx)
            block_den = jnp.sum(e, axis=0)
            block_acc = jnp.sum(e * v_ref[0, i, tok], axis=0)
            if mx is None:
                den, acc = block_den, block_acc
            else:
                alpha = jnp.exp(mx - new_mx)
                den = den * alpha + block_den
                acc = acc * alpha + block_acc
            mx = new_mx
        o_ref[i] = acc / den


def _sample_attention_specs(q_tiles, cache_k, layer, n_steps):
    n, rows, lanes = q_tiles.shape
    m = cache_k.shape[2]
    per_step = n // (n_steps[0] * n_steps[1])
    step = lambda i, s: i * n_steps[1] + s
    qspec = pl.BlockSpec((per_step, rows, lanes), lambda i, s: (step(i, s), 0, 0))
    kvspec = pl.BlockSpec((1, per_step, m, rows, lanes), lambda i, s: (layer, step(i, s), 0, 0, 0))
    return qspec, kvspec


def _even_prompt_kernel(x_ref, g_ref, w_in_ref, maps_ref, pscale_ref, sgw_ref, sgbias_ref, sgg_ref,
                        w_out_ref, sq_ref, sk_ref, sv_ref, xo_ref, tail_ref, so_ref,
                        xa_ref, ga_ref, u_ref, v_ref, gb_ref, y_ref, hist_ref):
    rows, w_a = y_ref.shape[0], pscale_ref.shape[-1]
    w_b = sgg_ref.shape[-1]
    gw = w_a // POOL_GROUPS
    sw = w_b // SGU_GROUPS
    s = pl.program_id(1)

    @pl.when(s == 0)
    def _():
        hist_ref[...] = jnp.zeros_like(hist_ref)

    x = x_ref[0]
    h = _rms(x, g_ref[0]).astype(BF16)
    xa_ref[...] = _dot(h, w_in_ref[0, :, 0:w_a])
    v_ref[...] = _dot(h, w_in_ref[0, :, 2 * w_a + w_b:2 * w_a + 2 * w_b])

    pos = s * rows + lax.broadcasted_iota(jnp.int32, (rows, gw), 0)
    pooled = []
    for g, w in enumerate(POOL_WINDOWS):
        cols = slice(g * gw, (g + 1) * gw)
        xa = xa_ref[:, cols]
        acc = jnp.concatenate([hist_ref[:, cols], xa], axis=0)
        shift = 1
        while shift < w:
            acc = acc + pltpu.roll(acc, shift, 0)
            shift *= 2
        cnt = jnp.minimum(pos + 1, w).astype(F32)
        pooled.append((acc[POOL_HIST:] / cnt - xa).astype(BF16))
    tail = xa_ref[rows - POOL_HIST:, :]
    hist_ref[...] = tail
    tail_ref[0] = tail

    ga_ref[...] = _dot(h, w_in_ref[0, :, w_a:2 * w_a])
    for g in range(POOL_GROUPS):
        cols = slice(g * gw, (g + 1) * gw)
        ya = _dot(pooled[g], maps_ref[0, g]) * pscale_ref[0, :, cols] * _silu(ga_ref[:, cols])
        y_ref[:, cols] = ya.astype(BF16)

    u_ref[...] = _dot(h, w_in_ref[0, :, 2 * w_a:2 * w_a + w_b])
    gb_ref[...] = _dot(h, w_in_ref[0, :, 2 * w_a + 2 * w_b:])
    out = x + _dot(y_ref[:, 0:w_a], w_out_ref[0, 0:w_a, :])

    vn = _rms(v_ref[...], sgg_ref[0]).astype(BF16)
    causal = (lax.broadcasted_iota(jnp.int32, (CHUNK, CHUNK), 0)
              >= lax.broadcasted_iota(jnp.int32, (CHUNK, CHUNK), 1))
    for g in range(SGU_GROUPS):
        cols = slice(g * sw, (g + 1) * sw)
        wg = jnp.where(causal, sgw_ref[0, g], 0.0).astype(BF16)
        for c in range(rows // CHUNK):
            rs = slice(c * CHUNK, (c + 1) * CHUNK)
            mixed = _dot(wg, vn[rs, cols]) + sgbias_ref[0, :, cols]
            y_ref[rs, w_a + g * sw:w_a + (g + 1) * sw] = (
                u_ref[rs, cols] * mixed * _silu(gb_ref[rs, cols])).astype(BF16)

    xo_ref[0] = out + _dot(y_ref[:, w_a:], w_out_ref[0, w_a:, :])
    _sample_attention(sq_ref, sk_ref, sv_ref, so_ref)


def _even_prompt(x, layer, j, norm_g, w_in, maps, pscale, sgw, sgbias, sgg, w_out, sq, cache_k, cache_v):
    b, seq, d = x.shape
    w_a, w_b = pscale.shape[-1], sgg.shape[-1]
    rows = PROMPT_ROWS
    xspec = pl.BlockSpec((1, rows, d), lambda i, s: (i, s, 0))
    qspec, kvspec = _sample_attention_specs(sq, cache_k, layer, (b, seq // rows))
    return pl.pallas_call(
        _even_prompt_kernel,
        grid=(b, seq // rows),
        in_specs=[xspec, _layer_block(norm_g, layer), _layer_block(w_in, 0), _layer_block(maps, 0),
                  _layer_block(pscale, j), _layer_block(sgw, j), _layer_block(sgbias, j),
                  _layer_block(sgg, j), _layer_block(w_out, 0), qspec, kvspec, kvspec],
        out_specs=[xspec, pl.BlockSpec((1, POOL_HIST, w_a), lambda i, s: (i, 0, 0)), qspec],
        out_shape=[jax.ShapeDtypeStruct(x.shape, F32),
                   jax.ShapeDtypeStruct((b, POOL_HIST, w_a), F32),
                   jax.ShapeDtypeStruct(sq.shape, F32)],
        scratch_shapes=[pltpu.VMEM((rows, w_a), F32), pltpu.VMEM((rows, w_a), F32),
                        pltpu.VMEM((rows, w_b), F32), pltpu.VMEM((rows, w_b), F32),
                        pltpu.VMEM((rows, w_b), F32),
                        pltpu.VMEM((rows, w_a + w_b), BF16),
                        pltpu.VMEM((POOL_HIST, w_a), F32)],
        compiler_params=_params(2),
        name="even_prompt",
    )(x, norm_g, w_in, maps, pscale, sgw, sgbias, sgg, w_out, sq, cache_k, cache_v)


def _odd_prompt_kernel(x_ref, g_ref, w_in_ref, cw_ref, w_out_ref, sq_ref, sk_ref, sv_ref,
                       xo_ref, tail_ref, so_ref, y_ref, hist_ref, *, col_slab):
    rows, w_c = y_ref.shape
    s = pl.program_id(1)

    @pl.when(s == 0)
    def _():
        hist_ref[...] = jnp.zeros_like(hist_ref)

    x = x_ref[0]
    h = _rms(x, g_ref[0]).astype(BF16)
    for c0 in range(0, w_c, col_slab):
        cols = slice(c0, c0 + col_slab)
        zb = [_dot(h, w_in_ref[0, :, part * w_c + c0:part * w_c + c0 + col_slab]) for part in range(4)]
        e = zb[1] * zb[2]
        ext = jnp.concatenate([hist_ref[:, cols], e], axis=0)
        y = (cw_ref[0, 0:1, cols] * pltpu.roll(ext, 2, 0)[CONV_HIST:]
             + cw_ref[0, 1:2, cols] * pltpu.roll(ext, 1, 0)[CONV_HIST:]
             + cw_ref[0, 2:3, cols] * e)
        tail = e[rows - CONV_HIST:]
        hist_ref[:, cols] = tail
        tail_ref[0, :, cols] = tail
        y_ref[:, cols] = (zb[0] * y * _silu(zb[3])).astype(BF16)
    xo_ref[0] = x + _dot(y_ref[...], w_out_ref[0])
    _sample_attention(sq_ref, sk_ref, sv_ref, so_ref)


def _odd_prompt(x, layer, j, norm_g, w_in, conv_w, w_out, sq, cache_k, cache_v):
    b, seq, d = x.shape
    w_c = conv_w.shape[-1]
    rows = PROMPT_ROWS
    col_slab = 512
    xspec = pl.BlockSpec((1, rows, d), lambda i, s: (i, s, 0))
    qspec, kvspec = _sample_attention_specs(sq, cache_k, layer, (b, seq // rows))
    return pl.pallas_call(
        functools.partial(_odd_prompt_kernel, col_slab=col_slab),
        grid=(b, seq // rows),
        in_specs=[xspec, _layer_block(norm_g, layer), _layer_block(w_in, 0), _layer_block(conv_w, j),
                  _layer_block(w_out, 0), qspec, kvspec, kvspec],
        out_specs=[xspec, pl.BlockSpec((1, CONV_HIST, w_c), lambda i, s: (i, 0, 0)), qspec],
        out_shape=[jax.ShapeDtypeStruct(x.shape, F32),
                   jax.ShapeDtypeStruct((b, CONV_HIST, w_c), F32),
                   jax.ShapeDtypeStruct(sq.shape, F32)],
        scratch_shapes=[pltpu.VMEM((rows, w_c), BF16),
                        pltpu.VMEM((CONV_HIST, w_c), F32)],
        compiler_params=_params(2),
        name="odd_prompt",
    )(x, norm_g, w_in, conv_w, w_out, sq, cache_k, cache_v)


def _attn_prompt_kernel(*refs, final, n_cast):
    x_ref, g_ref, wq_ref, kt_ref, v_ref, wo_ref, gf_ref = refs[:7]
    cast_src = refs[7:7 + n_cast]
    xo_ref = refs[7 + n_cast]
    cast_dst = refs[8 + n_cast:8 + 2 * n_cast]
    o_ref = refs[8 + 2 * n_cast]
    _cast_slabs(cast_src, cast_dst)
    d = x_ref.shape[-1]
    hd = d // XA_HEADS
    x = x_ref[0]
    h = _rms(x, g_ref[0]).astype(BF16)
    q = _dot(h, wq_ref[0]).astype(BF16)
    heads = [slice(hh * hd, (hh + 1) * hd) for hh in range(XA_HEADS)]
    scores = [_dot(q[:, cols], kt_ref[0, 0, cols, :]) * (hd ** -0.5) for cols in heads]
    xn = x
    for hh, (cols, s) in enumerate(zip(heads, scores)):
        e = jnp.exp(s - jnp.max(s, axis=-1, keepdims=True))
        p = e / jnp.sum(e, axis=-1, keepdims=True)
        o_ref[:, cols] = _dot(p.astype(BF16), v_ref[0, 0, :, cols]).astype(BF16)
        if hh % 2 == 1:
            pair = slice((hh - 1) * hd, (hh + 1) * hd)
            xn = xn + _dot(o_ref[:, pair], wo_ref[0, pair, :])
    if final:
        xn = _rms(xn, gf_ref[...])
    xo_ref[0] = xn


def _attn_prompt(x, layer, norm_g, w_q, kt, vb, w_o, final_g, final, to_cast):
    b, seq, d = x.shape
    m = vb.shape[2]
    rows = ATTN_ROWS
    n_steps = b * (seq // rows)
    xspec = pl.BlockSpec((1, rows, d), lambda i, s: (i, s, 0))
    cast_in, cast_out, cast_shapes = _cast_specs(to_cast, n_steps, lambda i, s: i * (seq // rows) + s)
    outs = pl.pallas_call(
        functools.partial(_attn_prompt_kernel, final=final, n_cast=len(to_cast)),
        grid=(b, seq // rows),
        in_specs=[xspec, _layer_block(norm_g, layer), _layer_block(w_q, 0),
                  pl.BlockSpec((1, 1, d, m), lambda i, s: (layer, i, 0, 0)),
                  pl.BlockSpec((1, 1, m, d), lambda i, s: (layer, i, 0, 0)),
                  _layer_block(w_o, 0),
                  pl.BlockSpec((1, d), lambda i, s: (0, 0), pipeline_mode=pl.Buffered(1))] + cast_in,
        out_specs=[xspec] + cast_out,
        out_shape=[jax.ShapeDtypeStruct(x.shape, F32)] + cast_shapes,
        scratch_shapes=[pltpu.VMEM((rows, d), BF16)],
        compiler_params=_params(2),
        name="attn_prompt",
    )(x, norm_g, w_q, kt, vb, w_o, final_g, *[w for w, _ in to_cast])
    return outs[0], outs[1:]


def _sample_enter(refs, first):
    if first:
        return refs[0][...], refs[1:]
    xs_ref, o_ref, wo_ref = refs[:3]
    return _sample_residual(xs_ref[...], o_ref, wo_ref), refs[3:]


def _sample_residual(xs, o_ref, wo_ref):
    n, d = xs.shape
    rows = o_ref.shape[0] // n
    hd = d // XA_HEADS
    acc = xs
    for r in range(rows):
        f = _tile_row_feature(r, hd)
        o_r = o_ref[pl.ds(r, n, stride=rows), :].astype(BF16)
        acc = acc + _dot(o_r, wo_ref[0, f:f + LANES, :])
    return acc


def _sample_queries(xs, g_ref, wq_ref, q_ref):
    q = _dot(_rms(xs, g_ref[0]).astype(BF16), wq_ref[0])
    q_ref[...] = _to_tile_rows(q, q_ref.shape[1])


def _carry_operand(carry):
    if carry is None:
        return [], ()
    zeros = (0,) * carry.ndim
    return [pl.BlockSpec(carry.shape, lambda i: zeros)], (carry,)


def _split_carry(refs, n_carry):
    return (refs[0], refs[1:]) if n_carry else (None, refs)


def _prev_attention_specs(o_rows, w_o_prev):
    return [pl.BlockSpec(o_rows.shape, lambda i: (0, 0)), _layer_block(w_o_prev, 0)]


def _even_sample_kernel(*refs, pos0, first, n_carry):
    xs, refs = _sample_enter(refs, first)
    (st_ref, g_ref, w_in_ref, maps_ref, pscale_ref, w00_ref, b0_ref, sgg_ref, w_out_ref, gx_ref,
     wq_ref) = refs[:11]
    carry_ref, refs = _split_carry(refs[11:], n_carry)
    xo_ref, pool_ref, vn_ref, q_ref, y_ref = refs
    w_a, w_b = pscale_ref.shape[-1], sgg_ref.shape[-1]
    gw = w_a // POOL_GROUPS
    h = _rms(xs, g_ref[0]).astype(BF16)
    xa = _dot(h, w_in_ref[0, :, 0:w_a])
    ga = _dot(h, w_in_ref[0, :, w_a:2 * w_a])
    u = _dot(h, w_in_ref[0, :, 2 * w_a:2 * w_a + w_b])
    v = _dot(h, w_in_ref[0, :, 2 * w_a + w_b:2 * w_a + 2 * w_b])
    gb = _dot(h, w_in_ref[0, :, 2 * w_a + 2 * w_b:])

    for g, w in enumerate(POOL_WINDOWS):
        cols = slice(g * gw, (g + 1) * gw)
        acc = xa[:, cols]
        for k in range(POOL_CTX - (w - 1), POOL_CTX):
            acc = acc + st_ref[0, k, :, cols]
        pooled = acc / float(min(pos0 + 1, w)) - xa[:, cols]
        ya = _dot(pooled.astype(BF16), maps_ref[0, g]) * pscale_ref[0, :, cols] * _silu(ga[:, cols])
        y_ref[:, cols] = ya.astype(BF16)
    if n_carry:
        pool_ref[0:n_carry] = carry_ref[...]
    pool_ref[n_carry, 0:POOL_CTX - 1] = st_ref[0, 1:POOL_CTX]
    pool_ref[n_carry, POOL_CTX - 1] = xa

    vn = _rms(v, sgg_ref[0])
    vn_ref[0] = vn
    mixed = w00_ref[0] * vn + b0_ref[0]
    y_ref[:, w_a:] = (u * mixed * _silu(gb)).astype(BF16)
    xn = xs + _dot(y_ref[...], w_out_ref[0])
    xo_ref[...] = xn
    _sample_queries(xn, gx_ref, wq_ref, q_ref)


def _even_sample(xs, prev, layer, j, state, carry, norm_g, w_in, maps, pscale, w00, b0, sgg, w_out,
                 norm_xg, w_q, w_o, pos0):
    n, d = xs.shape
    _, ctx, _, w_a = state.shape
    w_b = sgg.shape[-1]
    whole = pl.BlockSpec((n, d), lambda i: (0, 0))
    qshape = (n, d // LANES, LANES)
    first = prev is None
    head_specs = [whole] + ([] if first else _prev_attention_specs(prev, w_o))
    head_args = (xs,) if first else (xs, prev, w_o)
    carry_specs, carry_args = _carry_operand(carry)
    return pl.pallas_call(
        functools.partial(_even_sample_kernel, pos0=pos0, first=first, n_carry=j),
        grid=(1,),
        in_specs=head_specs + [
            _layer_block(state, j), _layer_block(norm_g, layer), _layer_block(w_in, 0),
            _layer_block(maps, 0), _layer_block(pscale, j), _layer_block(w00, j), _layer_block(b0, j),
            _layer_block(sgg, j), _layer_block(w_out, 0), _layer_block(norm_xg, layer),
            _layer_block(w_q, 0)] + carry_specs,
        out_specs=[whole, pl.BlockSpec((j + 1, ctx, n, w_a), lambda i: (0, 0, 0, 0)),
                   pl.BlockSpec((1, n, w_b), lambda i: (0, 0, 0)),
                   pl.BlockSpec(qshape, lambda i: (0, 0, 0))],
        out_shape=[jax.ShapeDtypeStruct((n, d), F32),
                   jax.ShapeDtypeStruct((j + 1, ctx, n, w_a), F32),
                   jax.ShapeDtypeStruct((1, n, w_b), F32),
                   jax.ShapeDtypeStruct(qshape, F32)],
        scratch_shapes=[pltpu.VMEM((n, w_a + w_b), BF16)],
        compiler_params=_params(1),
        name="even_sample",
    )(*head_args, state, norm_g, w_in, maps, pscale, w00, b0, sgg, w_out, norm_xg, w_q, *carry_args)


def _odd_sample_kernel(*refs, first, n_carry):
    xs, refs = _sample_enter(refs, first)
    st_ref, g_ref, w_in_ref, cw_ref, w_out_ref, gx_ref, wq_ref = refs[:7]
    carry_ref, refs = _split_carry(refs[7:], n_carry)
    xo_ref, conv_ref, q_ref, y_ref = refs
    w_c = cw_ref.shape[-1]
    h = _rms(xs, g_ref[0]).astype(BF16)
    bg = _dot(h, w_in_ref[0, :, 0:w_c])
    cg = _dot(h, w_in_ref[0, :, w_c:2 * w_c])
    xc = _dot(h, w_in_ref[0, :, 2 * w_c:3 * w_c])
    gate = _dot(h, w_in_ref[0, :, 3 * w_c:])
    e = cg * xc
    newest = st_ref[0, :, CONV_CTX - 1, :]
    y = cw_ref[0, 0:1, :] * st_ref[0, :, 0, :] + cw_ref[0, 1:2, :] * newest + cw_ref[0, 2:3, :] * e
    if n_carry:
        conv_ref[0:n_carry] = carry_ref[...]
    conv_ref[n_carry, :, 0, :] = newest
    conv_ref[n_carry, :, CONV_CTX - 1, :] = e
    y_ref[...] = (bg * y * _silu(gate)).astype(BF16)
    xn = xs + _dot(y_ref[...], w_out_ref[0])
    xo_ref[...] = xn
    _sample_queries(xn, gx_ref, wq_ref, q_ref)


def _odd_sample(xs, prev, layer, j, state, carry, norm_g, w_in, conv_w, w_out, norm_xg, w_q, w_o):
    n, d = xs.shape
    w_c = conv_w.shape[-1]
    whole = pl.BlockSpec((n, d), lambda i: (0, 0))
    stspec = pl.BlockSpec((1, n, CONV_CTX, w_c), lambda i: (j, 0, 0, 0))
    qshape = (n, d // LANES, LANES)
    first = prev is None
    head_specs = [whole] + ([] if first else _prev_attention_specs(prev, w_o))
    head_args = (xs,) if first else (xs, prev, w_o)
    carry_specs, carry_args = _carry_operand(carry)
    return pl.pallas_call(
        functools.partial(_odd_sample_kernel, first=first, n_carry=j),
        grid=(1,),
        in_specs=head_specs + [
            stspec, _layer_block(norm_g, layer), _layer_block(w_in, 0), _layer_block(conv_w, j),
            _layer_block(w_out, 0), _layer_block(norm_xg, layer), _layer_block(w_q, 0)] + carry_specs,
        out_specs=[whole, pl.BlockSpec((j + 1, n, CONV_CTX, w_c), lambda i: (0, 0, 0, 0)),
                   pl.BlockSpec(qshape, lambda i: (0, 0, 0))],
        out_shape=[jax.ShapeDtypeStruct((n, d), F32),
                   jax.ShapeDtypeStruct((j + 1, n, CONV_CTX, w_c), F32),
                   jax.ShapeDtypeStruct(qshape, F32)],
        scratch_shapes=[pltpu.VMEM((n, w_c), BF16)],
        compiler_params=_params(1),
        name="odd_sample",
    )(*head_args, state, norm_g, w_in, conv_w, w_out, norm_xg, w_q, *carry_args)


def _sample_finish_kernel(xs_ref, o_ref, wo_ref, gf_ref, y_ref):
    y_ref[...] = _rms(_sample_residual(xs_ref[...], o_ref, wo_ref), gf_ref[...])


def _sample_finish(xs, prev, w_o, final_g):
    n, d = xs.shape
    whole = pl.BlockSpec((n, d), lambda i: (0, 0))
    return pl.pallas_call(
        _sample_finish_kernel,
        grid=(1,),
        in_specs=[whole] + _prev_attention_specs(prev, w_o)
                 + [pl.BlockSpec((1, d), lambda i: (0, 0))],
        out_specs=whole,
        out_shape=jax.ShapeDtypeStruct((n, d), F32),
        compiler_params=_params(1),
        name="sample_finish",
    )(xs, prev, w_o, final_g)


def kernel(x_prompt, x_sample, mem_prompt, state_pool, state_conv, cache_mem_k, cache_mem_v, norm_mix_g, norm_xattn_g, norm_mem_g, w_in_ab, pool_maps, pool_scale, sgu_w, sgu_b, sgu_g, w_out_ab, w_in_c, conv_w, w_out_c, w_q, w_k, w_v, w_o, norm_final_g):
    depth, d = norm_mix_g.shape
    n_s, dec_seq, _ = x_sample.shape
    n_even, n_odd = pool_scale.shape[0], conv_w.shape[0]
    w_a, w_b, w_c = pool_scale.shape[-1], sgu_g.shape[-1], conv_w.shape[-1]
    hd = d // XA_HEADS
    assert dec_seq == 1, "sample group is one new token per sequence"
    assert PAST_LEN % CHUNK == 0, "the sample token must open a spatial-gating chunk"
    assert (hd // LANES) * XA_HEADS == SUBLANES, "one memory token per (8, 128) register tile"

    maps_rows = pool_maps.reshape(n_even, w_a, w_a // POOL_GROUPS)

    def matmul_weights(i):
        j = i // 2
        mixer = [(w_in_ab, j), (w_out_ab, j), (maps_rows, j)] if i % 2 == 0 else [(w_in_c, j), (w_out_c, j)]
        return mixer + [(w_q, i), (w_o, i)]

    row = lambda a: a.reshape(a.shape[0], 1, a.shape[-1])
    g_mix, g_xattn, g_mem = row(norm_mix_g), row(norm_xattn_g), row(norm_mem_g)
    pscale, sgg = row(pool_scale), row(sgu_g)
    g_final = norm_final_g.reshape(1, d)
    sw = w_b // SGU_GROUPS
    sgbias = jnp.repeat(jnp.swapaxes(sgu_b, 1, 2), sw, axis=2)
    w00 = row(jnp.repeat(sgu_w[:, :, 0, 0], sw, axis=1))
    b0 = row(jnp.repeat(sgu_b[:, :, 0], sw, axis=1))

    (k_p, v_p, kt_p, vb_p), wts = _memory_kv(mem_prompt, g_mem, w_k, w_v, matmul_weights(0))
    cache_k = _lane_tiled(cache_mem_k)
    cache_v = _lane_tiled(cache_mem_v)
    pool_state = jnp.swapaxes(state_pool, 1, 2)

    xp = x_prompt
    xs = x_sample.reshape(n_s, d)
    so = wo_prev = None
    pool_s = conv_s = None
    pool_p, conv_p, vrows_s = [], [], []
    for i in range(depth):
        j = i // 2
        if i % 2 == 0:
            w_in_b, w_out_b, maps_b, wq_b, wo_b = wts
            maps_b = maps_b.reshape(1, POOL_GROUPS, w_a // POOL_GROUPS, w_a // POOL_GROUPS)
            xs, pool_s, vn, sq = _even_sample(xs, so, i, j, pool_state, pool_s, g_mix, w_in_b, maps_b,
                                                pscale, w00, b0, sgg, w_out_b, g_xattn, wq_b,
                                                wo_prev, PAST_LEN)
            vrows_s.append(vn)
            xp, tail, so = _even_prompt(xp, i, j, g_mix, w_in_b, maps_b, pscale, sgu_w, sgbias, sgg,
                                        w_out_b, sq, cache_k, cache_v)
            pool_p.append(tail[:, POOL_HIST - POOL_CTX:])
        else:
            w_in_b, w_out_b, wq_b, wo_b = wts
            xs, conv_s, sq = _odd_sample(xs, so, i, j, state_conv, conv_s, g_mix, w_in_b, conv_w,
                                           w_out_b, g_xattn, wq_b, wo_prev)
            xp, tail, so = _odd_prompt(xp, i, j, g_mix, w_in_b, conv_w, w_out_b, sq, cache_k, cache_v)
            conv_p.append(tail[:, CONV_HIST - CONV_CTX:])
        so = so.reshape(n_s * so.shape[1], LANES)
        wo_prev = wo_b
        xp, wts = _attn_prompt(xp, i, g_xattn, wq_b, kt_p, vb_p, wo_b, g_final, i == depth - 1,
                               matmul_weights(i + 1) if i + 1 < depth else [])
    ys = _sample_finish(xs, so, wo_prev, g_final)

    new_pool_s = jnp.swapaxes(pool_s, 1, 2)
    new_vrows = jnp.concatenate(vrows_s, axis=0).reshape(n_even, n_s, 1, w_b)
    return (xp, ys.reshape(n_s, 1, d), jnp.stack(pool_p), new_pool_s, jnp.stack(conv_p),
            conv_s, new_vrows, _from_lane_tiled(k_p), _from_lane_tiled(v_p))
```

```python
import functools

import jax
import jax.numpy as jnp
from jax import lax
from jax.experimental import pallas as pl
from jax.experimental.pallas import tpu as pltpu

POOL_WINDOWS = (2, 4, 8, 16)
POOL_GROUPS = len(POOL_WINDOWS)
POOL_CTX = max(POOL_WINDOWS) - 1
SGU_GROUPS = 4
CHUNK = 128
CONV_WIDTH = 3
CONV_CTX = CONV_WIDTH - 1
XA_HEADS = 4
EPS = 1e-6
PAST_LEN = 16384

V7X_VMEM_LIMIT_BYTES = 60 * 1024 * 1024
SUBLANES = 8
LANES = 128
POOL_HIST = 16
CONV_HIST = 8

PROMPT_ROWS = 512
ATTN_ROWS = 1024
MEMKV_BATCH = 2
SAMPLE_TOKEN_BLOCK = 32

F32 = jnp.float32
BF16 = jnp.bfloat16


def _rms(x, g):
    return x * lax.rsqrt(jnp.mean(x * x, axis=-1, keepdims=True) + EPS) * g


def _silu(x):
    return x * jax.nn.sigmoid(x)


def _dot(a, b):
    return jnp.dot(a, b, preferred_element_type=F32)


def _layer_block(arr, layer):
    index = (layer,) + (0,) * (arr.ndim - 1)
    return pl.BlockSpec((1,) + arr.shape[1:], lambda *_: index, pipeline_mode=pl.Buffered(1))


def _params(n_grid):
    return pltpu.CompilerParams(dimension_semantics=("arbitrary",) * n_grid,
                                vmem_limit_bytes=V7X_VMEM_LIMIT_BYTES)


def _tile_row_feature(r, hd):
    return (r % XA_HEADS) * hd + (r // XA_HEADS) * LANES


def _lane_tiled(cache):
    l, b, m, h, hd = cache.shape
    t = hd // LANES
    return cache.reshape(l, b, m, h, t, LANES).transpose(0, 1, 2, 4, 3, 5).reshape(l, b, m, t * h, LANES)


def _from_lane_tiled(tiled):
    l, b, m, r, lanes = tiled.shape
    t = r // XA_HEADS
    return tiled.reshape(l, b, m, t, XA_HEADS, lanes).transpose(0, 1, 2, 4, 3, 5).reshape(
        l, b, m, XA_HEADS, t * lanes)


def _to_tile_rows(a, rows):
    hd = a.shape[1] // XA_HEADS
    pieces = [a[:, _tile_row_feature(r, hd):_tile_row_feature(r, hd) + LANES] for r in range(rows)]
    return jnp.swapaxes(jnp.stack(pieces, axis=0), 0, 1)


def _cast_specs(to_cast, n_steps, step):
    ins, outs, shapes = [], [], []
    for w, l in to_cast:
        _, k, n = w.shape
        ins.append(pl.BlockSpec((1, k // n_steps, n), lambda i, s, l=l: (l, step(i, s), 0)))
        outs.append(pl.BlockSpec((1, k // n_steps, n), lambda i, s: (0, step(i, s), 0)))
        shapes.append(jax.ShapeDtypeStruct((1, k, n), BF16))
    return ins, outs, shapes


def _cast_slabs(srcs, dsts):
    for src, dst in zip(srcs, dsts):
        dst[...] = src[...].astype(BF16)


def _memkv_kernel(*refs, n_cast):
    mem_ref, g_ref, wk_ref, wv_ref = refs[:4]
    k_ref, v_ref, kt_ref, vb_ref = refs[4 + n_cast:8 + n_cast]
    _cast_slabs(refs[4:4 + n_cast], refs[8 + n_cast:])
    nb, m, d = mem_ref.shape
    rows = k_ref.shape[3]
    mn = _rms(mem_ref[...].reshape(nb * m, d), g_ref[0]).astype(BF16)
    k = _dot(mn, wk_ref[0].astype(BF16))
    v = _dot(mn, wv_ref[0].astype(BF16))
    for i in range(nb):
        tok = slice(i * m, (i + 1) * m)
        k_ref[0, i] = _to_tile_rows(k[tok], rows)
        v_ref[0, i] = _to_tile_rows(v[tok], rows)
        kt_ref[0, i] = k[tok].T.astype(BF16)
        vb_ref[0, i] = v[tok].astype(BF16)


def _memory_kv(mem, norm_g, w_k, w_v, to_cast):
    depth, _, d = norm_g.shape
    b, m, _ = mem.shape
    rows = d // LANES
    nb = MEMKV_BATCH
    wspec = pl.BlockSpec((1, d, d), lambda l, i: (l, 0, 0))
    tiled = pl.BlockSpec((1, nb, m, rows, LANES), lambda l, i: (l, i, 0, 0, 0))
    cast_in, cast_out, cast_shapes = _cast_specs(to_cast, depth * (b // nb), lambda l, i: l * (b // nb) + i)
    outs = pl.pallas_call(
        functools.partial(_memkv_kernel, n_cast=len(to_cast)),
        grid=(depth, b // nb),
        in_specs=[pl.BlockSpec((nb, m, d), lambda l, i: (i, 0, 0)),
                  pl.BlockSpec((1, 1, d), lambda l, i: (l, 0, 0)),
                  wspec, wspec] + cast_in,
        out_specs=[tiled, tiled,
                   pl.BlockSpec((1, nb, d, m), lambda l, i: (l, i, 0, 0)),
                   pl.BlockSpec((1, nb, m, d), lambda l, i: (l, i, 0, 0))] + cast_out,
        out_shape=[jax.ShapeDtypeStruct((depth, b, m, rows, LANES), F32),
                   jax.ShapeDtypeStruct((depth, b, m, rows, LANES), F32),
                   jax.ShapeDtypeStruct((depth, b, d, m), BF16),
                   jax.ShapeDtypeStruct((depth, b, m, d), BF16)] + cast_shapes,
        compiler_params=_params(2),
        name="memory_kv",
    )(mem, norm_g, w_k, w_v, *[w for w, _ in to_cast])
    return outs[:4], outs[4:]


def _sample_attention(q_ref, k_ref, v_ref, o_ref):
    n, rows, lanes = q_ref.shape
    m = k_ref.shape[2]
    hd = rows * lanes // XA_HEADS
    for i in range(n):
        qt = q_ref[i] * (hd ** -0.5)
        mx = den = acc = None
        for t0 in range(0, m, SAMPLE_TOKEN_BLOCK):
            tok = slice(t0, t0 + SAMPLE_TOKEN_BLOCK)
            part = jnp.sum(k_ref[0, i, tok] * qt[None], axis=-1, keepdims=True)
            s = part + pltpu.roll(part, XA_HEADS, 1)
            block_max = jnp.max(s, axis=0)
            new_mx = block_max if mx is None else jnp.maximum(mx, block_max)
            e = jnp.exp(s - new_mx)
            block_den = jnp.sum(e, axis=0)
            block_acc = jnp.sum(e * v_ref[0, i, tok], axis=0)
            if mx is None:
                den, acc = block_den, block_acc
            else:
                alpha = jnp.exp(mx - new_mx)
                den = den * alpha + block_den
                acc = acc * alpha + block_acc
            mx = new_mx
        o_ref[i] = acc / den


def _sample_attention_specs(q_tiles, cache_k, layer, n_steps):
    n, rows, lanes = q_tiles.shape
    m = cache_k.shape[2]
    per_step = n // (n_steps[0] * n_steps[1])
    step = lambda i, s: i * n_steps[1] + s
    qspec = pl.BlockSpec((per_step, rows, lanes), lambda i, s: (step(i, s), 0, 0))
    kvspec = pl.BlockSpec((1, per_step, m, rows, lanes), lambda i, s: (layer, step(i, s), 0, 0, 0))
    return qspec, kvspec


def _even_prompt_kernel(x_ref, g_ref, w_in_ref, maps_ref, pscale_ref, sgw_ref, sgbias_ref, sgg_ref,
                        w_out_ref, sq_ref, sk_ref, sv_ref, xo_ref, tail_ref, so_ref,
                        xa_ref, ga_ref, u_ref, v_ref, gb_ref, y_ref, hist_ref):
    rows, w_a = y_ref.shape[0], pscale_ref.shape[-1]
    w_b = sgg_ref.shape[-1]
    gw = w_a // POOL_GROUPS
    sw = w_b // SGU_GROUPS
    s = pl.program_id(1)

    @pl.when(s == 0)
    def _():
        hist_ref[...] = jnp.zeros_like(hist_ref)

    x = x_ref[0]
    h = _rms(x, g_ref[0]).astype(BF16)
    xa_ref[...] = _dot(h, w_in_ref[0, :, 0:w_a])
    v_ref[...] = _dot(h, w_in_ref[0, :, 2 * w_a + w_b:2 * w_a + 2 * w_b])

    pos = s * rows + lax.broadcasted_iota(jnp.int32, (rows, gw), 0)
    pooled = []
    for g, w in enumerate(POOL_WINDOWS):
        cols = slice(g * gw, (g + 1) * gw)
        xa = xa_ref[:, cols]
        acc = jnp.concatenate([hist_ref[:, cols], xa], axis=0)
        shift = 1
        while shift < w:
            acc = acc + pltpu.roll(acc, shift, 0)
            shift *= 2
        cnt = jnp.minimum(pos + 1, w).astype(F32)
        pooled.append((acc[POOL_HIST:] / cnt - xa).astype(BF16))
    tail = xa_ref[rows - POOL_HIST:, :]
    hist_ref[...] = tail
    tail_ref[0] = tail

    ga_ref[...] = _dot(h, w_in_ref[0, :, w_a:2 * w_a])
    for g in range(POOL_GROUPS):
        cols = slice(g * gw, (g + 1) * gw)
        ya = _dot(pooled[g], maps_ref[0, g]) * pscale_ref[0, :, cols] * _silu(ga_ref[:, cols])
        y_ref[:, cols] = ya.astype(BF16)

    u_ref[...] = _dot(h, w_in_ref[0, :, 2 * w_a:2 * w_a + w_b])
    gb_ref[...] = _dot(h, w_in_ref[0, :, 2 * w_a + 2 * w_b:])
    out = x + _dot(y_ref[:, 0:w_a], w_out_ref[0, 0:w_a, :])

    vn = _rms(v_ref[...], sgg_ref[0]).astype(BF16)
    causal = (lax.broadcasted_iota(jnp.int32, (CHUNK, CHUNK), 0)
              >= lax.broadcasted_iota(jnp.int32, (CHUNK, CHUNK), 1))
    for g in range(SGU_GROUPS):
        cols = slice(g * sw, (g + 1) * sw)
        wg = jnp.where(causal, sgw_ref[0, g], 0.0).astype(BF16)
        for c in range(rows // CHUNK):
            rs = slice(c * CHUNK, (c + 1) * CHUNK)
            mixed = _dot(wg, vn[rs, cols]) + sgbias_ref[0, :, cols]
            y_ref[rs, w_a + g * sw:w_a + (g + 1) * sw] = (
                u_ref[rs, cols] * mixed * _silu(gb_ref[rs, cols])).astype(BF16)

    xo_ref[0] = out + _dot(y_ref[:, w_a:], w_out_ref[0, w_a:, :])
    _sample_attention(sq_ref, sk_ref, sv_ref, so_ref)


def _even_prompt(x, layer, j, norm_g, w_in, maps, pscale, sgw, sgbias, sgg, w_out, sq, cache_k, cache_v):
    b, seq, d = x.shape
    w_a, w_b = pscale.shape[-1], sgg.shape[-1]
    rows = PROMPT_ROWS
    xspec = pl.BlockSpec((1, rows, d), lambda i, s: (i, s, 0))
    qspec, kvspec = _sample_attention_specs(sq, cache_k, layer, (b, seq // rows))
    return pl.pallas_call(
        _even_prompt_kernel,
        grid=(b, seq // rows),
        in_specs=[xspec, _layer_block(norm_g, layer), _layer_block(w_in, 0), _layer_block(maps, 0),
                  _layer_block(pscale, j), _layer_block(sgw, j), _layer_block(sgbias, j),
                  _layer_block(sgg, j), _layer_block(w_out, 0), qspec, kvspec, kvspec],
        out_specs=[xspec, pl.BlockSpec((1, POOL_HIST, w_a), lambda i, s: (i, 0, 0)), qspec],
        out_shape=[jax.ShapeDtypeStruct(x.shape, F32),
                   jax.ShapeDtypeStruct((b, POOL_HIST, w_a), F32),
                   jax.ShapeDtypeStruct(sq.shape, F32)],
        scratch_shapes=[pltpu.VMEM((rows, w_a), F32), pltpu.VMEM((rows, w_a), F32),
                        pltpu.VMEM((rows, w_b), F32), pltpu.VMEM((rows, w_b), F32),
                        pltpu.VMEM((rows, w_b), F32),
                        pltpu.VMEM((rows, w_a + w_b), BF16),
                        pltpu.VMEM((POOL_HIST, w_a), F32)],
        compiler_params=_params(2),
        name="even_prompt",
    )(x, norm_g, w_in, maps, pscale, sgw, sgbias, sgg, w_out, sq, cache_k, cache_v)


def _odd_prompt_kernel(x_ref, g_ref, w_in_ref, cw_ref, w_out_ref, sq_ref, sk_ref, sv_ref,
                       xo_ref, tail_ref, so_ref, y_ref, hist_ref, *, col_slab):
    rows, w_c = y_ref.shape
    s = pl.program_id(1)

    @pl.when(s == 0)
    def _():
        hist_ref[...] = jnp.zeros_like(hist_ref)

    x = x_ref[0]
    h = _rms(x, g_ref[0]).astype(BF16)
    for c0 in range(0, w_c, col_slab):
        cols = slice(c0, c0 + col_slab)
        zb = [_dot(h, w_in_ref[0, :, part * w_c + c0:part * w_c + c0 + col_slab]) for part in range(4)]
        e = zb[1] * zb[2]
        ext = jnp.concatenate([hist_ref[:, cols], e], axis=0)
        y = (cw_ref[0, 0:1, cols] * pltpu.roll(ext, 2, 0)[CONV_HIST:]
             + cw_ref[0, 1:2, cols] * pltpu.roll(ext, 1, 0)[CONV_HIST:]
             + cw_ref[0, 2:3, cols] * e)
        tail = e[rows - CONV_HIST:]
        hist_ref[:, cols] = tail
        tail_ref[0, :, cols] = tail
        y_ref[:, cols] = (zb[0] * y * _silu(zb[3])).astype(BF16)
    xo_ref[0] = x + _dot(y_ref[...], w_out_ref[0])
    _sample_attention(sq_ref, sk_ref, sv_ref, so_ref)


def _odd_prompt(x, layer, j, norm_g, w_in, conv_w, w_out, sq, cache_k, cache_v):
    b, seq, d = x.shape
    w_c = conv_w.shape[-1]
    rows = PROMPT_ROWS
    col_slab = 512
    xspec = pl.BlockSpec((1, rows, d), lambda i, s: (i, s, 0))
    qspec, kvspec = _sample_attention_specs(sq, cache_k, layer, (b, seq // rows))
    return pl.pallas_call(
        functools.partial(_odd_prompt_kernel, col_slab=col_slab),
        grid=(b, seq // rows),
        in_specs=[xspec, _layer_block(norm_g, layer), _layer_block(w_in, 0), _layer_block(conv_w, j),
                  _layer_block(w_out, 0), qspec, kvspec, kvspec],
        out_specs=[xspec, pl.BlockSpec((1, CONV_HIST, w_c), lambda i, s: (i, 0, 0)), qspec],
        out_shape=[jax.ShapeDtypeStruct(x.shape, F32),
                   jax.ShapeDtypeStruct((b, CONV_HIST, w_c), F32),
                   jax.ShapeDtypeStruct(sq.shape, F32)],
        scratch_shapes=[pltpu.VMEM((rows, w_c), BF16),
                        pltpu.VMEM((CONV_HIST, w_c), F32)],
        compiler_params=_params(2),
        name="odd_prompt",
    )(x, norm_g, w_in, conv_w, w_out, sq, cache_k, cache_v)


def _attn_prompt_kernel(*refs, final, n_cast):
    x_ref, g_ref, wq_ref, kt_ref, v_ref, wo_ref, gf_ref = refs[:7]
    cast_src = refs[7:7 + n_cast]
    xo_ref = refs[7 + n_cast]
    cast_dst = refs[8 + n_cast:8 + 2 * n_cast]
    o_ref = refs[8 + 2 * n_cast]
    _cast_slabs(cast_src, cast_dst)
    d = x_ref.shape[-1]
    hd = d // XA_HEADS
    x = x_ref[0]
    h = _rms(x, g_ref[0]).astype(BF16)
    q = _dot(h, wq_ref[0]).astype(BF16)
    heads = [slice(hh * hd, (hh + 1) * hd) for hh in range(XA_HEADS)]
    scores = [_dot(q[:, cols], kt_ref[0, 0, cols, :]) * (hd ** -0.5) for cols in heads]
    xn = x
    for hh, (cols, s) in enumerate(zip(heads, scores)):
        e = jnp.exp(s - jnp.max(s, axis=-1, keepdims=True))
        p = e / jnp.sum(e, axis=-1, keepdims=True)
        o_ref[:, cols] = _dot(p.astype(BF16), v_ref[0, 0, :, cols]).astype(BF16)
        xn = xn + _dot(o_ref[:, cols], wo_ref[0, cols, :])
    if final:
        xn = _rms(xn, gf_ref[...])
    xo_ref[0] = xn


def _attn_prompt(x, layer, norm_g, w_q, kt, vb, w_o, final_g, final, to_cast):
    b, seq, d = x.shape
    m = vb.shape[2]
    rows = ATTN_ROWS
    n_steps = b * (seq // rows)
    xspec = pl.BlockSpec((1, rows, d), lambda i, s: (i, s, 0))
    cast_in, cast_out, cast_shapes = _cast_specs(to_cast, n_steps, lambda i, s: i * (seq // rows) + s)
    outs = pl.pallas_call(
        functools.partial(_attn_prompt_kernel, final=final, n_cast=len(to_cast)),
        grid=(b, seq // rows),
        in_specs=[xspec, _layer_block(norm_g, layer), _layer_block(w_q, 0),
                  pl.BlockSpec((1, 1, d, m), lambda i, s: (layer, i, 0, 0)),
                  pl.BlockSpec((1, 1, m, d), lambda i, s: (layer, i, 0, 0)),
                  _layer_block(w_o, 0),
                  pl.BlockSpec((1, d), lambda i, s: (0, 0), pipeline_mode=pl.Buffered(1))] + cast_in,
        out_specs=[xspec] + cast_out,
        out_shape=[jax.ShapeDtypeStruct(x.shape, F32)] + cast_shapes,
        scratch_shapes=[pltpu.VMEM((rows, d), BF16)],
        compiler_params=_params(2),
        name="attn_prompt",
    )(x, norm_g, w_q, kt, vb, w_o, final_g, *[w for w, _ in to_cast])
    return outs[0], outs[1:]


def _sample_enter(refs, first):
    if first:
        return refs[0][...], refs[1:]
    xs_ref, o_ref, wo_ref = refs[:3]
    return _sample_residual(xs_ref[...], o_ref, wo_ref), refs[3:]


def _sample_residual(xs, o_ref, wo_ref):
    n, d = xs.shape
    rows = o_ref.shape[0] // n
    hd = d // XA_HEADS
    acc = xs
    for r in range(rows):
        f = _tile_row_feature(r, hd)
        o_r = o_ref[pl.ds(r, n, stride=rows), :].astype(BF16)
        acc = acc + _dot(o_r, wo_ref[0, f:f + LANES, :])
    return acc


def _sample_queries(xs, g_ref, wq_ref, q_ref):
    q = _dot(_rms(xs, g_ref[0]).astype(BF16), wq_ref[0])
    q_ref[...] = _to_tile_rows(q, q_ref.shape[1])


def _carry_operand(carry):
    if carry is None:
        return [], ()
    zeros = (0,) * carry.ndim
    return [pl.BlockSpec(carry.shape, lambda i: zeros)], (carry,)


def _split_carry(refs, n_carry):
    return (refs[0], refs[1:]) if n_carry else (None, refs)


def _prev_attention_specs(o_rows, w_o_prev):
    return [pl.BlockSpec(o_rows.shape, lambda i: (0, 0)), _layer_block(w_o_prev, 0)]


def _even_sample_kernel(*refs, pos0, first, n_carry):
    xs, refs = _sample_enter(refs, first)
    (st_ref, g_ref, w_in_ref, maps_ref, pscale_ref, w00_ref, b0_ref, sgg_ref, w_out_ref, gx_ref,
     wq_ref) = refs[:11]
    carry_ref, refs = _split_carry(refs[11:], n_carry)
    xo_ref, pool_ref, vn_ref, q_ref, y_ref = refs
    w_a, w_b = pscale_ref.shape[-1], sgg_ref.shape[-1]
    gw = w_a // POOL_GROUPS
    h = _rms(xs, g_ref[0]).astype(BF16)
    xa = _dot(h, w_in_ref[0, :, 0:w_a])
    ga = _dot(h, w_in_ref[0, :, w_a:2 * w_a])
    u = _dot(h, w_in_ref[0, :, 2 * w_a:2 * w_a + w_b])
    v = _dot(h, w_in_ref[0, :, 2 * w_a + w_b:2 * w_a + 2 * w_b])
    gb = _dot(h, w_in_ref[0, :, 2 * w_a + 2 * w_b:])

    for g, w in enumerate(POOL_WINDOWS):
        cols = slice(g * gw, (g + 1) * gw)
        acc = xa[:, cols]
        for k in range(POOL_CTX - (w - 1), POOL_CTX):
            acc = acc + st_ref[0, k, :, cols]
        pooled = acc / float(min(pos0 + 1, w)) - xa[:, cols]
        ya = _dot(pooled.astype(BF16), maps_ref[0, g]) * pscale_ref[0, :, cols] * _silu(ga[:, cols])
        y_ref[:, cols] = ya.astype(BF16)
    if n_carry:
        pool_ref[0:n_carry] = carry_ref[...]
    pool_ref[n_carry, 0:POOL_CTX - 1] = st_ref[0, 1:POOL_CTX]
    pool_ref[n_carry, POOL_CTX - 1] = xa

    vn = _rms(v, sgg_ref[0])
    vn_ref[0] = vn
    mixed = w00_ref[0] * vn + b0_ref[0]
    y_ref[:, w_a:] = (u * mixed * _silu(gb)).astype(BF16)
    xn = xs + _dot(y_ref[...], w_out_ref[0])
    xo_ref[...] = xn
    _sample_queries(xn, gx_ref, wq_ref, q_ref)


def _even_sample(xs, prev, layer, j, state, carry, norm_g, w_in, maps, pscale, w00, b0, sgg, w_out,
                 norm_xg, w_q, w_o, pos0):
    n, d = xs.shape
    _, ctx, _, w_a = state.shape
    w_b = sgg.shape[-1]
    whole = pl.BlockSpec((n, d), lambda i: (0, 0))
    qshape = (n, d // LANES, LANES)
    first = prev is None
    head_specs = [whole] + ([] if first else _prev_attention_specs(prev, w_o))
    head_args = (xs,) if first else (xs, prev, w_o)
    carry_specs, carry_args = _carry_operand(carry)
    return pl.pallas_call(
        functools.partial(_even_sample_kernel, pos0=pos0, first=first, n_carry=j),
        grid=(1,),
        in_specs=head_specs + [
            _layer_block(state, j), _layer_block(norm_g, layer), _layer_block(w_in, 0),
            _layer_block(maps, 0), _layer_block(pscale, j), _layer_block(w00, j), _layer_block(b0, j),
            _layer_block(sgg, j), _layer_block(w_out, 0), _layer_block(norm_xg, layer),
            _layer_block(w_q, 0)] + carry_specs,
        out_specs=[whole, pl.BlockSpec((j + 1, ctx, n, w_a), lambda i: (0, 0, 0, 0)),
                   pl.BlockSpec((1, n, w_b), lambda i: (0, 0, 0)),
                   pl.BlockSpec(qshape, lambda i: (0, 0, 0))],
        out_shape=[jax.ShapeDtypeStruct((n, d), F32),
                   jax.ShapeDtypeStruct((j + 1, ctx, n, w_a), F32),
                   jax.ShapeDtypeStruct((1, n, w_b), F32),
                   jax.ShapeDtypeStruct(qshape, F32)],
        scratch_shapes=[pltpu.VMEM((n, w_a + w_b), BF16)],
        compiler_params=_params(1),
        name="even_sample",
    )(*head_args, state, norm_g, w_in, maps, pscale, w00, b0, sgg, w_out, norm_xg, w_q, *carry_args)


def _odd_sample_kernel(*refs, first, n_carry):
    xs, refs = _sample_enter(refs, first)
    st_ref, g_ref, w_in_ref, cw_ref, w_out_ref, gx_ref, wq_ref = refs[:7]
    carry_ref, refs = _split_carry(refs[7:], n_carry)
    xo_ref, conv_ref, q_ref, y_ref = refs
    w_c = cw_ref.shape[-1]
    h = _rms(xs, g_ref[0]).astype(BF16)
    bg = _dot(h, w_in_ref[0, :, 0:w_c])
    cg = _dot(h, w_in_ref[0, :, w_c:2 * w_c])
    xc = _dot(h, w_in_ref[0, :, 2 * w_c:3 * w_c])
    gate = _dot(h, w_in_ref[0, :, 3 * w_c:])
    e = cg * xc
    newest = st_ref[0, :, CONV_CTX - 1, :]
    y = cw_ref[0, 0:1, :] * st_ref[0, :, 0, :] + cw_ref[0, 1:2, :] * newest + cw_ref[0, 2:3, :] * e
    if n_carry:
        conv_ref[0:n_carry] = carry_ref[...]
    conv_ref[n_carry, :, 0, :] = newest
    conv_ref[n_carry, :, CONV_CTX - 1, :] = e
    y_ref[...] = (bg * y * _silu(gate)).astype(BF16)
    xn = xs + _dot(y_ref[...], w_out_ref[0])
    xo_ref[...] = xn
    _sample_queries(xn, gx_ref, wq_ref, q_ref)


def _odd_sample(xs, prev, layer, j, state, carry, norm_g, w_in, conv_w, w_out, norm_xg, w_q, w_o):
    n, d = xs.shape
    w_c = conv_w.shape[-1]
    whole = pl.BlockSpec((n, d), lambda i: (0, 0))
    stspec = pl.BlockSpec((1, n, CONV_CTX, w_c), lambda i: (j, 0, 0, 0))
    qshape = (n, d // LANES, LANES)
    first = prev is None
    head_specs = [whole] + ([] if first else _prev_attention_specs(prev, w_o))
    head_args = (xs,) if first else (xs, prev, w_o)
    carry_specs, carry_args = _carry_operand(carry)
    return pl.pallas_call(
        functools.partial(_odd_sample_kernel, first=first, n_carry=j),
        grid=(1,),
        in_specs=head_specs + [
            stspec, _layer_block(norm_g, layer), _layer_block(w_in, 0), _layer_block(conv_w, j),
            _layer_block(w_out, 0), _layer_block(norm_xg, layer), _layer_block(w_q, 0)] + carry_specs,
        out_specs=[whole, pl.BlockSpec((j + 1, n, CONV_CTX, w_c), lambda i: (0, 0, 0, 0)),
                   pl.BlockSpec(qshape, lambda i: (0, 0, 0))],
        out_shape=[jax.ShapeDtypeStruct((n, d), F32),
                   jax.ShapeDtypeStruct((j + 1, n, CONV_CTX, w_c), F32),
                   jax.ShapeDtypeStruct(qshape, F32)],
        scratch_shapes=[pltpu.VMEM((n, w_c), BF16)],
        compiler_params=_params(1),
        name="odd_sample",
    )(*head_args, state, norm_g, w_in, conv_w, w_out, norm_xg, w_q, *carry_args)


def _sample_finish_kernel(xs_ref, o_ref, wo_ref, gf_ref, y_ref):
    y_ref[...] = _rms(_sample_residual(xs_ref[...], o_ref, wo_ref), gf_ref[...])


def _sample_finish(xs, prev, w_o, final_g):
    n, d = xs.shape
    whole = pl.BlockSpec((n, d), lambda i: (0, 0))
    return pl.pallas_call(
        _sample_finish_kernel,
        grid=(1,),
        in_specs=[whole] + _prev_attention_specs(prev, w_o)
                 + [pl.BlockSpec((1, d), lambda i: (0, 0))],
        out_specs=whole,
        out_shape=jax.ShapeDtypeStruct((n, d), F32),
        compiler_params=_params(1),
        name="sample_finish",
    )(xs, prev, w_o, final_g)


def kernel(x_prompt, x_sample, mem_prompt, state_pool, state_conv, cache_mem_k, cache_mem_v, norm_mix_g, norm_xattn_g, norm_mem_g, w_in_ab, pool_maps, pool_scale, sgu_w, sgu_b, sgu_g, w_out_ab, w_in_c, conv_w, w_out_c, w_q, w_k, w_v, w_o, norm_final_g):
    depth, d = norm_mix_g.shape
    n_s, dec_seq, _ = x_sample.shape
    n_even, n_odd = pool_scale.shape[0], conv_w.shape[0]
    w_a, w_b, w_c = pool_scale.shape[-1], sgu_g.shape[-1], conv_w.shape[-1]
    hd = d // XA_HEADS
    assert dec_seq == 1, "sample group is one new token per sequence"
    assert PAST_LEN % CHUNK == 0, "the sample token must open a spatial-gating chunk"
    assert (hd // LANES) * XA_HEADS == SUBLANES, "one memory token per (8, 128) register tile"

    maps_rows = pool_maps.reshape(n_even, w_a, w_a // POOL_GROUPS)

    def matmul_weights(i):
        j = i // 2
        mixer = [(w_in_ab, j), (w_out_ab, j), (maps_rows, j)] if i % 2 == 0 else [(w_in_c, j), (w_out_c, j)]
        return mixer + [(w_q, i), (w_o, i)]

    row = lambda a: a.reshape(a.shape[0], 1, a.shape[-1])
    g_mix, g_xattn, g_mem = row(norm_mix_g), row(norm_xattn_g), row(norm_mem_g)
    pscale, sgg = row(pool_scale), row(sgu_g)
    g_final = norm_final_g.reshape(1, d)
    sw = w_b // SGU_GROUPS
    sgbias = jnp.repeat(jnp.swapaxes(sgu_b, 1, 2), sw, axis=2)
    w00 = row(jnp.repeat(sgu_w[:, :, 0, 0], sw, axis=1))
    b0 = row(jnp.repeat(sgu_b[:, :, 0], sw, axis=1))

    (k_p, v_p, kt_p, vb_p), wts = _memory_kv(mem_prompt, g_mem, w_k, w_v, matmul_weights(0))
    cache_k = _lane_tiled(cache_mem_k)
    cache_v = _lane_tiled(cache_mem_v)
    pool_state = jnp.swapaxes(state_pool, 1, 2)

    xp = x_prompt
    xs = x_sample.reshape(n_s, d)
    so = wo_prev = None
    pool_s = conv_s = None
    pool_p, conv_p, vrows_s = [], [], []
    for i in range(depth):
        j = i // 2
        if i % 2 == 0:
            w_in_b, w_out_b, maps_b, wq_b, wo_b = wts
            maps_b = maps_b.reshape(1, POOL_GROUPS, w_a // POOL_GROUPS, w_a // POOL_GROUPS)
            xs, pool_s, vn, sq = _even_sample(xs, so, i, j, pool_state, pool_s, g_mix, w_in_b, maps_b,
                                                pscale, w00, b0, sgg, w_out_b, g_xattn, wq_b,
                                                wo_prev, PAST_LEN)
            vrows_s.append(vn)
            xp, tail, so = _even_prompt(xp, i, j, g_mix, w_in_b, maps_b, pscale, sgu_w, sgbias, sgg,
                                        w_out_b, sq, cache_k, cache_v)
            pool_p.append(tail[:, POOL_HIST - POOL_CTX:])
        else:
            w_in_b, w_out_b, wq_b, wo_b = wts
            xs, conv_s, sq = _odd_sample(xs, so, i, j, state_conv, conv_s, g_mix, w_in_b, conv_w,
                                           w_out_b, g_xattn, wq_b, wo_prev)
            xp, tail, so = _odd_prompt(xp, i, j, g_mix, w_in_b, conv_w, w_out_b, sq, cache_k, cache_v)
            conv_p.append(tail[:, CONV_HIST - CONV_CTX:])
        so = so.reshape(n_s * so.shape[1], LANES)
        wo_prev = wo_b
        xp, wts = _attn_prompt(xp, i, g_xattn, wq_b, kt_p, vb_p, wo_b, g_final, i == depth - 1,
                               matmul_weights(i + 1) if i + 1 < depth else [])
    ys = _sample_finish(xs, so, wo_prev, g_final)

    new_pool_s = jnp.swapaxes(pool_s, 1, 2)
    new_vrows = jnp.concatenate(vrows_s, axis=0).reshape(n_even, n_s, 1, w_b)
    return (xp, ys.reshape(n_s, 1, d), jnp.stack(pool_p), new_pool_s, jnp.stack(conv_p),
            conv_s, new_vrows, _from_lane_tiled(k_p), _from_lane_tiled(v_p))
```
